```python
import jax, jax.numpy as jnp
from jax import lax
import numpy as np

D_MODEL = 1024
BATCH = 4
SEQ = 4096
DEPTH = 4
DEC_BATCH = 128
DEC_SEQ = 8
PAST_LEN = 8192
PAGE_SIZE = 128

N_HEADS = 16
HEAD_DIM = 64
N_KV_HEADS = 4
GROUP = N_HEADS // N_KV_HEADS
Q_DIM = N_HEADS * HEAD_DIM
KV_DIM = N_KV_HEADS * HEAD_DIM
QKV_DIM = Q_DIM + 2 * KV_DIM
ATTN_SCALE = HEAD_DIM ** -0.5
Q_BLOCK = 128
WINDOW = 128
D_FF = 2816
N_EXPERTS = 8
TOP_K = 2
D_FF_EXPERT = 3584
PLE_DIM = 256
N_MIXERS = 2
N_SB_LAYERS = (DEPTH + 1) // 2
N_SWA_LAYERS = DEPTH // 2
RMS_EPS = 1e-6
SB_BIAS_INIT = -7.0

kernel_name = "stickbreak_swa_sink_hybrid_step"


def rms_norm(x, g):
    xf = x.astype(jnp.float32)
    y = xf * lax.rsqrt(jnp.mean(xf * xf, axis=-1, keepdims=True) + RMS_EPS)
    return (y * g.astype(jnp.float32)).astype(x.dtype)


def alibi_slopes():
    h = jnp.arange(1, N_HEADS + 1, dtype=jnp.float32)
    return jnp.exp2(-8.0 * h / N_HEADS).reshape(N_KV_HEADS, GROUP)


def split_qkv(t):
    lead = t.shape[:-1]
    q = t[..., :Q_DIM].reshape(*lead, N_KV_HEADS, GROUP, HEAD_DIM)
    k = t[..., Q_DIM:Q_DIM + KV_DIM].reshape(*lead, N_KV_HEADS, HEAD_DIM)
    v = t[..., Q_DIM + KV_DIM:].reshape(*lead, N_KV_HEADS, HEAD_DIM)
    return q, k, v


def stick_breaking(q, k, v, q_pos, k_pos, bias):
    z = jnp.einsum('bqkgd,bskd->bkgqs', q, k).astype(jnp.float32) * ATTN_SCALE
    z = z + bias.astype(jnp.float32).reshape(N_KV_HEADS, GROUP)[:, :, None, None]
    mask = k_pos[None, :] < q_pos[:, None]
    log_one_minus = jnp.where(mask, jax.nn.log_sigmoid(-z), 0.0)
    tail = lax.cumsum(log_one_minus, axis=4, reverse=True) - log_one_minus
    a = jnp.where(mask, jnp.exp(jax.nn.log_sigmoid(z) + tail), 0.0)
    return jnp.einsum('bkgqs,bskd->bqkgd', a, v.astype(jnp.float32))


def stick_breaking_prompt(q, k, v, bias):
    b, s = k.shape[0], k.shape[1]
    nb = s // Q_BLOCK
    qb = jnp.moveaxis(q.reshape(b, nb, Q_BLOCK, N_KV_HEADS, GROUP, HEAD_DIM), 1, 0)
    pos = jnp.arange(s, dtype=jnp.int32)
    o = lax.map(lambda a: stick_breaking(a[0], k, v, a[1], pos, bias), (qb, pos.reshape(nb, Q_BLOCK)))
    return jnp.moveaxis(o, 0, 1).reshape(b, s, Q_DIM)


def stick_breaking_sample(q, k_new, v_new, pool_k, pool_v, page_table, layer, bias):
    db, t = q.shape[0], q.shape[1]
    past = page_table.shape[1] * PAGE_SIZE
    k_past = pool_k[layer][page_table].reshape(db, past, N_KV_HEADS, HEAD_DIM)
    v_past = pool_v[layer][page_table].reshape(db, past, N_KV_HEADS, HEAD_DIM)
    k_all = jnp.concatenate([k_past, k_new.astype(k_past.dtype)], axis=1)
    v_all = jnp.concatenate([v_past, v_new.astype(v_past.dtype)], axis=1)
    q_pos = past + jnp.arange(t, dtype=jnp.int32)
    k_pos = jnp.arange(past + t, dtype=jnp.int32)
    return stick_breaking(q, k_all, v_all, q_pos, k_pos, bias).reshape(db, t, Q_DIM)


def swa_attend(q, k, v, dist, valid, sinks):
    s = jnp.einsum('...qkgd,...skd->...kgqs', q, k).astype(jnp.float32) * ATTN_SCALE
    s = s - alibi_slopes()[:, :, None, None] * dist.astype(jnp.float32)
    s = jnp.where(valid, s, -jnp.inf)
    sink = jnp.broadcast_to(sinks.astype(jnp.float32).reshape(N_KV_HEADS, GROUP)[:, :, None, None],
                            s.shape[:-1] + (1,))
    probs = jax.nn.softmax(jnp.concatenate([s, sink], axis=-1), axis=-1)[..., :-1]
    return jnp.einsum('...kgqs,...skd->...qkgd', probs, v.astype(jnp.float32))


def swa_prompt(q, k, v, sinks):
    b, s = k.shape[0], k.shape[1]
    nb = s // WINDOW
    qb = q.reshape(b, nb, WINDOW, N_KV_HEADS, GROUP, HEAD_DIM)

    def band(x):
        xb = x.reshape(b, nb, WINDOW, N_KV_HEADS, HEAD_DIM)
        prev = jnp.pad(xb, ((0, 0), (1, 0), (0, 0), (0, 0), (0, 0)))[:, :-1]
        return jnp.concatenate([prev, xb], axis=2)

    qi = jnp.arange(WINDOW)[:, None]
    kj = jnp.arange(2 * WINDOW)[None, :]
    dist = WINDOW + qi - kj
    blk = jnp.arange(nb)[:, None, None]
    valid = (dist >= 0) & (dist < WINDOW) & ((kj >= WINDOW) | (blk > 0))
    o = swa_attend(qb, band(k), band(v), dist, valid[:, None, None], sinks)
    w_keep = min(WINDOW, s)
    return o.reshape(b, s, Q_DIM), k[:, s - w_keep:], v[:, s - w_keep:]


def swa_sample(q, k_new, v_new, buf_k, buf_v, sinks):
    db, t = q.shape[0], q.shape[1]
    w_buf = buf_k.shape[1]
    k_all = jnp.concatenate([buf_k, k_new.astype(buf_k.dtype)], axis=1)
    v_all = jnp.concatenate([buf_v, v_new.astype(buf_v.dtype)], axis=1)
    dist = (w_buf + jnp.arange(t))[:, None] - jnp.arange(w_buf + t)[None, :]
    valid = (dist >= 0) & (dist < WINDOW)
    o = swa_attend(q, k_all, v_all, dist, valid, sinks)
    return o.reshape(db, t, Q_DIM), k_all[:, t:], v_all[:, t:]


def swiglu(x, w_gu, w_down):
    gu = x @ w_gu
    half = w_gu.shape[-1] // 2
    return (jax.nn.silu(gu[..., :half]) * gu[..., half:]) @ w_down


def moe_swiglu(h, w_router, w_gu, w_down):
    x = h.reshape(-1, D_MODEL)
    logits = (x @ w_router).astype(jnp.float32)
    top_v, top_i = lax.top_k(logits, TOP_K)
    g = jax.nn.softmax(top_v, axis=-1)
    gates = jnp.einsum('nk,nke->ne', g, jax.nn.one_hot(top_i, N_EXPERTS, dtype=jnp.float32))
    out = jnp.zeros(x.shape, jnp.float32)
    for e in range(N_EXPERTS):
        out = out + gates[:, e:e + 1] * swiglu(x, w_gu[e], w_down[e]).astype(jnp.float32)
    return out.astype(h.dtype).reshape(h.shape)


def ple_add(h, p, norm_g, w_proj, w_gate):
    gate = jax.nn.sigmoid((rms_norm(h, norm_g) @ w_gate).astype(jnp.float32))
    return h + ((p @ w_proj).astype(jnp.float32) * gate).astype(h.dtype)


def setup_inputs(seed: int = 0) -> dict:
    key = jax.random.key(seed)
    ks = jax.random.split(key, 24)
    n_pages = PAST_LEN // PAGE_SIZE
    n_used = DEC_BATCH * n_pages
    n_phys = n_used + max(1, n_used // 4)
    w_buf = min(WINDOW, PAST_LEN)

    def nrm(k, shape, scale=1.0):
        return jax.random.normal(k, shape, jnp.float32) * scale

    page_table = jax.random.permutation(ks[0], n_phys)[:n_used].reshape(DEC_BATCH, n_pages).astype(jnp.int32)
    return {
        "x_prompt": nrm(ks[1], (BATCH, SEQ, D_MODEL)),
        "x_sample": nrm(ks[2], (DEC_BATCH, DEC_SEQ, D_MODEL)),
        "cache_sb_k": nrm(ks[3], (N_SB_LAYERS, n_phys, PAGE_SIZE, N_KV_HEADS, HEAD_DIM)),
        "cache_sb_v": nrm(ks[4], (N_SB_LAYERS, n_phys, PAGE_SIZE, N_KV_HEADS, HEAD_DIM)),
        "cache_swa_k": nrm(ks[5], (N_SWA_LAYERS, DEC_BATCH, w_buf, N_KV_HEADS, HEAD_DIM)),
        "cache_swa_v": nrm(ks[6], (N_SWA_LAYERS, DEC_BATCH, w_buf, N_KV_HEADS, HEAD_DIM)),
        "page_table": page_table,
        "p_prompt": nrm(ks[7], (DEPTH, BATCH, SEQ, PLE_DIM)),
        "p_sample": nrm(ks[8], (DEPTH, DEC_BATCH, DEC_SEQ, PLE_DIM)),
        "attn_norm": 1.0 + nrm(ks[9], (DEPTH, D_MODEL), 0.05),
        "w_qkv": nrm(ks[10], (DEPTH, D_MODEL, QKV_DIM), D_MODEL ** -0.5),
        "w_o": nrm(ks[11], (DEPTH, Q_DIM, D_MODEL), Q_DIM ** -0.5),
        "sb_bias": SB_BIAS_INIT + nrm(ks[23], (N_SB_LAYERS, N_HEADS), 0.1),
        "attn_sinks": nrm(ks[12], (N_SWA_LAYERS, N_HEADS), 0.5),
        "ffn_norm": 1.0 + nrm(ks[13], (DEPTH, D_MODEL), 0.05),
        "w_gu_dense": nrm(ks[14], (N_SB_LAYERS, D_MODEL, 2 * D_FF), D_MODEL ** -0.5),
        "w_down_dense": nrm(ks[15], (N_SB_LAYERS, D_FF, D_MODEL), D_FF ** -0.5),
        "w_router": nrm(ks[16], (N_SWA_LAYERS, D_MODEL, N_EXPERTS), D_MODEL ** -0.5),
        "w_gu_expert": nrm(ks[17], (N_SWA_LAYERS, N_EXPERTS, D_MODEL, 2 * D_FF_EXPERT), D_MODEL ** -0.5),
        "w_down_expert": nrm(ks[18], (N_SWA_LAYERS, N_EXPERTS, D_FF_EXPERT, D_MODEL), D_FF_EXPERT ** -0.5),
        "ple_norm": 1.0 + nrm(ks[19], (DEPTH, D_MODEL), 0.05),
        "w_ple_proj": nrm(ks[20], (DEPTH, PLE_DIM, D_MODEL), PLE_DIM ** -0.5),
        "w_ple_gate": nrm(ks[21], (DEPTH, D_MODEL, D_MODEL), D_MODEL ** -0.5),
        "final_norm": 1.0 + nrm(ks[22], (D_MODEL,), 0.05),
    }


def reference(x_prompt, x_sample, cache_sb_k, cache_sb_v, cache_swa_k, cache_swa_v, page_table,
              p_prompt, p_sample, attn_norm, w_qkv, w_o, sb_bias, attn_sinks, ffn_norm, w_gu_dense,
              w_down_dense, w_router, w_gu_expert, w_down_expert, ple_norm, w_ple_proj,
              w_ple_gate, final_norm):
    hp, hs = x_prompt, x_sample
    sb_kp, sb_vp, sb_ks, sb_vs = [], [], [], []
    sw_kp, sw_vp, sw_ks, sw_vs = [], [], [], []
    for i in range(DEPTH):
        j = i // N_MIXERS
        qp, kp, vp = split_qkv(rms_norm(hp, attn_norm[i]) @ w_qkv[i])
        qs, ks_, vs = split_qkv(rms_norm(hs, attn_norm[i]) @ w_qkv[i])
        if i % N_MIXERS == 0:
            op = stick_breaking_prompt(qp, kp, vp, sb_bias[j])
            os_ = stick_breaking_sample(qs, ks_, vs, cache_sb_k, cache_sb_v, page_table, j, sb_bias[j])
            sb_kp.append(kp); sb_vp.append(vp); sb_ks.append(ks_); sb_vs.append(vs)
        else:
            op, bkp, bvp = swa_prompt(qp, kp, vp, attn_sinks[j])
            os_, bks, bvs = swa_sample(qs, ks_, vs, cache_swa_k[j], cache_swa_v[j], attn_sinks[j])
            sw_kp.append(bkp); sw_vp.append(bvp); sw_ks.append(bks); sw_vs.append(bvs)
        hp = hp + op.astype(hp.dtype) @ w_o[i]
        hs = hs + os_.astype(hs.dtype) @ w_o[i]
        if i % 2 == 0:
            hp = hp + swiglu(rms_norm(hp, ffn_norm[i]), w_gu_dense[j], w_down_dense[j])
            hs = hs + swiglu(rms_norm(hs, ffn_norm[i]), w_gu_dense[j], w_down_dense[j])
        else:
            hp = hp + moe_swiglu(rms_norm(hp, ffn_norm[i]), w_router[j], w_gu_expert[j], w_down_expert[j])
            hs = hs + moe_swiglu(rms_norm(hs, ffn_norm[i]), w_router[j], w_gu_expert[j], w_down_expert[j])
        hp = ple_add(hp, p_prompt[i], ple_norm[i], w_ple_proj[i], w_ple_gate[i])
        hs = ple_add(hs, p_sample[i], ple_norm[i], w_ple_proj[i], w_ple_gate[i])
    y_prompt = rms_norm(hp, final_norm)
    y_sample = rms_norm(hs, final_norm)
    return (y_prompt, y_sample,
            jnp.stack(sb_kp), jnp.stack(sb_vp), jnp.stack(sb_ks), jnp.stack(sb_vs),
            jnp.stack(sw_kp), jnp.stack(sw_vp), jnp.stack(sw_ks), jnp.stack(sw_vs))
```

```python
import functools

import numpy as np
import jax
import jax.numpy as jnp
from jax import lax
from jax.experimental import pallas as pl
from jax.experimental.pallas import tpu as pltpu

F32 = jnp.float32
BF16 = jnp.bfloat16

D_MODEL = 1024
N_HEADS = 16
HEAD_DIM = 64
N_KV_HEADS = 4
GROUP = N_HEADS // N_KV_HEADS
Q_DIM = N_HEADS * HEAD_DIM
KV_DIM = N_KV_HEADS * HEAD_DIM
QKV_DIM = Q_DIM + 2 * KV_DIM
ATTN_SCALE = HEAD_DIM ** -0.5
WINDOW = 128
PAGE_SIZE = 128
N_EXPERTS = 8
RMS_EPS = 1e-6

LANES = 128
ROUTER_LANES = LANES
VMEM_LIMIT = 56 * 1024 * 1024
PAGES_PER_STEP = 8
SWA_SEQS_PER_STEP = 8
NEG_INF = float("-inf")


def _cparams(sem, vmem=VMEM_LIMIT):
    return pltpu.CompilerParams(dimension_semantics=sem, vmem_limit_bytes=vmem)


def _dot(a, b):
    return jnp.dot(a, b, preferred_element_type=F32)


def _dot_nt(a, b):
    return lax.dot_general(a, b, (((1,), (1,)), ((), ())), preferred_element_type=F32)


def _rms(x, g):
    ms = jnp.mean(x * x, axis=-1, keepdims=True)
    return x * lax.rsqrt(ms + RMS_EPS) * g


def _qkv_kernel(h_ref, g_ref, w_ref, q_ref, k_ref, v_ref, kb_ref, vb_ref):
    xn = _rms(h_ref[...], g_ref[...]).astype(BF16)
    t = _dot(xn, w_ref[...])
    q_ref[...] = (t[:, :Q_DIM] * ATTN_SCALE).astype(BF16)
    k = t[:, Q_DIM:Q_DIM + KV_DIM]
    v = t[:, Q_DIM + KV_DIM:]
    k_ref[...] = k
    v_ref[...] = v
    kb_ref[...] = k.astype(BF16)
    vb_ref[...] = v.astype(BF16)


def qkv_proj(h, g, w, tm):
    n = h.shape[0]
    row = lambda i: (i, 0)
    fixed = lambda i: (0, 0)
    return pl.pallas_call(
        _qkv_kernel,
        grid=(n // tm,),
        in_specs=[pl.BlockSpec((tm, D_MODEL), row),
                  pl.BlockSpec((1, D_MODEL), fixed),
                  pl.BlockSpec((D_MODEL, QKV_DIM), fixed)],
        out_specs=[pl.BlockSpec((tm, Q_DIM), row),
                   pl.BlockSpec((tm, KV_DIM), row),
                   pl.BlockSpec((tm, KV_DIM), row),
                   pl.BlockSpec((tm, KV_DIM), row),
                   pl.BlockSpec((tm, KV_DIM), row)],
        out_shape=[jax.ShapeDtypeStruct((n, Q_DIM), BF16),
                   jax.ShapeDtypeStruct((n, KV_DIM), F32),
                   jax.ShapeDtypeStruct((n, KV_DIM), F32),
                   jax.ShapeDtypeStruct((n, KV_DIM), BF16),
                   jax.ShapeDtypeStruct((n, KV_DIM), BF16)],
        compiler_params=_cparams(("parallel",)),
        name="qkv_proj",
    )(h, g, w)


def _route(xn, wh_ref, wl_ref):
    xh = xn.astype(BF16)
    xl = (xn - xh.astype(F32)).astype(BF16)
    wh = wh_ref[...]
    logits = _dot(xh, wh) + _dot(xl, wh) + _dot(xh, wl_ref[...])
    lane = lax.broadcasted_iota(jnp.int32, logits.shape, 1)
    logits = jnp.where(lane < N_EXPERTS, logits, NEG_INF)
    m1 = jnp.max(logits, axis=1, keepdims=True)
    i1 = jnp.min(jnp.where(logits == m1, lane, ROUTER_LANES), axis=1, keepdims=True)
    rest = jnp.where(lane == i1, NEG_INF, logits)
    m2 = jnp.max(rest, axis=1, keepdims=True)
    i2 = jnp.min(jnp.where(rest == m2, lane, ROUTER_LANES), axis=1, keepdims=True)
    e2 = jnp.exp(m2 - m1)
    inv = 1.0 / (1.0 + e2)
    return jnp.where(lane == i1, inv, jnp.where(lane == i2, e2 * inv, 0.0))


def _post_attn_kernel(h_ref, a_ref, wo_ref, g_ref, *rest, routed):
    if routed:
        wh_ref, wl_ref, h1_ref, xn_ref, gates_ref = rest
    else:
        h1_ref, xn_ref = rest
    h1 = h_ref[...] + _dot(a_ref[...], wo_ref[...])
    h1_ref[...] = h1
    xn = _rms(h1, g_ref[...])
    xn_ref[...] = xn.astype(BF16)
    if routed:
        gates_ref[...] = _route(xn, wh_ref, wl_ref)


def post_attn(h, attn, wo, g, router, tm):
    n = h.shape[0]
    row = lambda i: (i, 0)
    fixed = lambda i: (0, 0)
    routed = router is not None
    in_specs = [pl.BlockSpec((tm, D_MODEL), row),
                pl.BlockSpec((tm, Q_DIM), row),
                pl.BlockSpec((Q_DIM, D_MODEL), fixed),
                pl.BlockSpec((1, D_MODEL), fixed)]
    out_specs = [pl.BlockSpec((tm, D_MODEL), row), pl.BlockSpec((tm, D_MODEL), row)]
    out_shape = [jax.ShapeDtypeStruct((n, D_MODEL), F32), jax.ShapeDtypeStruct((n, D_MODEL), BF16)]
    args = [h, attn, wo, g]
    if routed:
        in_specs += [pl.BlockSpec((D_MODEL, ROUTER_LANES), fixed)] * 2
        out_specs.append(pl.BlockSpec((tm, ROUTER_LANES), row))
        out_shape.append(jax.ShapeDtypeStruct((n, ROUTER_LANES), F32))
        args += list(router)
    return pl.pallas_call(
        functools.partial(_post_attn_kernel, routed=routed),
        grid=(n // tm,),
        in_specs=in_specs, out_specs=out_specs, out_shape=out_shape,
        compiler_params=_cparams(("parallel",)),
        name="post_attn_routed" if routed else "post_attn",
    )(*args)


def _swiglu_partial(x, wg, wu, wd):
    g = _dot(x, wg)
    u = _dot(x, wu)
    return _dot((g * jax.nn.sigmoid(g) * u).astype(BF16), wd)


def _ffn_kernel(h_ref, x_ref, wg_ref, wu_ref, wd_ref, o_ref, acc_ref):
    f = pl.program_id(1)

    @pl.when(f == 0)
    def _():
        acc_ref[...] = jnp.zeros_like(acc_ref)

    acc_ref[...] += _swiglu_partial(x_ref[...], wg_ref[...], wu_ref[...], wd_ref[...])

    @pl.when(f == pl.num_programs(1) - 1)
    def _():
        o_ref[...] = h_ref[...] + acc_ref[...]


def dense_ffn(h, xn, w_gu, w_down, tm, tf):
    n = h.shape[0]
    d_ff = w_down.shape[0]
    nf = d_ff // tf
    row = lambda i, f: (i, 0)
    return pl.pallas_call(
        _ffn_kernel,
        grid=(n // tm, nf),
        in_specs=[pl.BlockSpec((tm, D_MODEL), row),
                  pl.BlockSpec((tm, D_MODEL), row),
                  pl.BlockSpec((D_MODEL, tf), lambda i, f: (0, f)),
                  pl.BlockSpec((D_MODEL, tf), lambda i, f: (0, nf + f)),
                  pl.BlockSpec((tf, D_MODEL), lambda i, f: (f, 0))],
        out_specs=pl.BlockSpec((tm, D_MODEL), row),
        out_shape=jax.ShapeDtypeStruct((n, D_MODEL), F32),
        scratch_shapes=[pltpu.VMEM((tm, D_MODEL), F32)],
        compiler_params=_cparams(("parallel", "arbitrary")),
        name="dense_ffn",
    )(h, xn, w_gu, w_gu, w_down)


def _moe_dense_kernel(h_ref, x_ref, gates_ref, wg_ref, wu_ref, wd_ref, o_ref, acc_ref):
    e = pl.program_id(1)
    f = pl.program_id(2)

    @pl.when((e == 0) & (f == 0))
    def _():
        acc_ref[...] = jnp.zeros_like(acc_ref)

    gates = gates_ref[...]
    lane = lax.broadcasted_iota(jnp.int32, gates.shape, 1)
    gate = jnp.sum(jnp.where(lane == e, gates, 0.0), axis=1, keepdims=True)
    part = _swiglu_partial(x_ref[...], wg_ref[0], wu_ref[0], wd_ref[0])
    acc_ref[...] += gate * part

    @pl.when((e == pl.num_programs(1) - 1) & (f == pl.num_programs(2) - 1))
    def _():
        o_ref[...] = h_ref[...] + acc_ref[...]


def moe_dense(h, xn, gates, w_gu, w_down, tm, tf):
    n = h.shape[0]
    d_ff = w_down.shape[1]
    nf = d_ff // tf
    row = lambda i, e, f: (i, 0)
    return pl.pallas_call(
        _moe_dense_kernel,
        grid=(n // tm, N_EXPERTS, nf),
        in_specs=[pl.BlockSpec((tm, D_MODEL), row),
                  pl.BlockSpec((tm, D_MODEL), row),
                  pl.BlockSpec((tm, ROUTER_LANES), row),
                  pl.BlockSpec((1, D_MODEL, tf), lambda i, e, f: (e, 0, f)),
                  pl.BlockSpec((1, D_MODEL, tf), lambda i, e, f: (e, 0, nf + f)),
                  pl.BlockSpec((1, tf, D_MODEL), lambda i, e, f: (e, f, 0))],
        out_specs=pl.BlockSpec((tm, D_MODEL), row),
        out_shape=jax.ShapeDtypeStruct((n, D_MODEL), F32),
        scratch_shapes=[pltpu.VMEM((tm, D_MODEL), F32)],
        compiler_params=_cparams(("parallel", "arbitrary", "arbitrary")),
        name="moe_dense",
    )(h, xn, gates, w_gu, w_gu, w_down)


def _ple_kernel(h_ref, p_ref, g_ref, wp_ref, wg_ref, *rest, final):
    if final:
        gf_ref, o_ref = rest
    else:
        (o_ref,) = rest
    h = h_ref[...]
    gate = jax.nn.sigmoid(_dot(_rms(h, g_ref[...]).astype(BF16), wg_ref[...]))
    out = h + _dot(p_ref[...].astype(BF16), wp_ref[...]) * gate
    if final:
        out = _rms(out, gf_ref[...])
    o_ref[...] = out


def ple_add(h, p, g, w_proj, w_gate, final_g, tm):
    n = h.shape[0]
    ple_dim = p.shape[1]
    row = lambda i: (i, 0)
    fixed = lambda i: (0, 0)
    final = final_g is not None
    in_specs = [pl.BlockSpec((tm, D_MODEL), row),
                pl.BlockSpec((tm, ple_dim), row),
                pl.BlockSpec((1, D_MODEL), fixed),
                pl.BlockSpec((ple_dim, D_MODEL), fixed),
                pl.BlockSpec((D_MODEL, D_MODEL), fixed)]
    args = [h, p, g, w_proj, w_gate]
    if final:
        in_specs.append(pl.BlockSpec((1, D_MODEL), fixed))
        args.append(final_g)
    return pl.pallas_call(
        functools.partial(_ple_kernel, final=final),
        grid=(n // tm,),
        in_specs=in_specs,
        out_specs=pl.BlockSpec((tm, D_MODEL), row),
        out_shape=jax.ShapeDtypeStruct((n, D_MODEL), F32),
        compiler_params=_cparams(("parallel",)),
        name="ple_final" if final else "ple",
    )(*args)


def _cumsum_matrix(tk):
    j = np.arange(2 * tk)[:, None] % tk
    s = np.arange(2 * tk)[None, :]
    m = np.where(s < tk, j > s, True)
    return jnp.asarray(m, dtype=BF16)


def _sb_block(q4, k, v, bias, carry, m2, mask):
    tk = k.shape[0]
    z = _dot_nt(q4, k) + bias
    sp = jnp.maximum(z, 0.0) + jnp.log(1.0 + jnp.exp(-jnp.abs(z)))
    if mask is not None:
        sp = jnp.where(mask, sp, 0.0)
    sp_hi = sp.astype(BF16)
    sp_lo = (sp - sp_hi.astype(F32)).astype(BF16)
    res = _dot(jnp.concatenate([sp_hi, sp_lo], axis=1), m2)
    a = jnp.exp(z - sp - res[:, :tk] - carry)
    if mask is not None:
        a = jnp.where(mask, a, 0.0)
    return _dot(a.astype(BF16), v), res[:, tk:]


def _head_cols(h):
    return slice(h * HEAD_DIM, (h + 1) * HEAD_DIM)


def _sb_prompt_kernel(bias_ref, q_ref, k_ref, v_ref, m2_ref, o_ref,
                      q4_ref, bt_ref, acc_ref, car_ref, *, tq):
    qi = pl.program_id(1)
    rows = GROUP * tq
    row_g = lax.broadcasted_iota(jnp.int32, (rows, tq), 0) // tq
    for kvh in range(N_KV_HEADS):
        q4_ref[kvh] = jnp.concatenate(
            [q_ref[:, _head_cols(kvh * GROUP + g)] for g in range(GROUP)], axis=0)
        bt = jnp.zeros((rows, tq), F32)
        for g in range(GROUP):
            bt = jnp.where(row_g == g, bias_ref[kvh * GROUP + g], bt)
        bt_ref[kvh] = bt
    acc_ref[...] = jnp.zeros_like(acc_ref)
    car_ref[...] = jnp.zeros_like(car_ref)

    def block(j, mask):
        start = pl.multiple_of(j * tq, tq)
        for kvh in range(N_KV_HEADS):
            k = k_ref[pl.ds(start, tq), _head_cols(kvh)]
            v = v_ref[pl.ds(start, tq), _head_cols(kvh)]
            pv, inc = _sb_block(q4_ref[kvh], k, v, bt_ref[kvh], car_ref[kvh], m2_ref[...], mask)
            acc_ref[kvh] += pv
            car_ref[kvh] += inc

    t_in = lax.broadcasted_iota(jnp.int32, (rows, tq), 0) % tq
    s_in = lax.broadcasted_iota(jnp.int32, (rows, tq), 1)
    block(qi, s_in < t_in)

    def body(i, c):
        block(qi - 1 - i, None)
        return c

    lax.fori_loop(0, qi, body, 0)

    for kvh in range(N_KV_HEADS):
        for g in range(GROUP):
            o_ref[:, _head_cols(kvh * GROUP + g)] = acc_ref[kvh, g * tq:(g + 1) * tq, :].astype(BF16)


def sb_prompt(q, kb, vb, bias, batch, seq, tq):
    nq = seq // tq
    rows = GROUP * tq
    return pl.pallas_call(
        functools.partial(_sb_prompt_kernel, tq=tq),
        grid=(batch, nq),
        in_specs=[pl.BlockSpec(memory_space=pltpu.SMEM),
                  pl.BlockSpec((tq, Q_DIM), lambda b, i: (b * nq + i, 0)),
                  pl.BlockSpec((seq, KV_DIM), lambda b, i: (b, 0)),
                  pl.BlockSpec((seq, KV_DIM), lambda b, i: (b, 0)),
                  pl.BlockSpec((2 * tq, 2 * tq), lambda b, i: (0, 0))],
        out_specs=pl.BlockSpec((tq, Q_DIM), lambda b, i: (b * nq + i, 0)),
        out_shape=jax.ShapeDtypeStruct((batch * seq, Q_DIM), BF16),
        scratch_shapes=[pltpu.VMEM((N_KV_HEADS, rows, HEAD_DIM), BF16),
                        pltpu.VMEM((N_KV_HEADS, rows, tq), F32),
                        pltpu.VMEM((N_KV_HEADS, rows, HEAD_DIM), F32),
                        pltpu.VMEM((N_KV_HEADS, rows, tq), F32)],
        compiler_params=_cparams(("parallel", "parallel")),
        name="sb_prompt",
    )(bias, q, kb, vb, _cumsum_matrix(tq))


def _stage_sample_q(q_ref, q4_ref, t):
    q = q_ref[...].astype(F32)
    for kvh in range(N_KV_HEADS):
        q4_ref[kvh] = jnp.concatenate(
            [q[:, _head_cols(kvh * GROUP + g)] for g in range(GROUP)], axis=0).astype(BF16)


def _sb_sample_kernel(pt_ref, bias_ref, q_ref, kn_ref, vn_ref, m2_ref, *rest, t, npg):
    k_refs = rest[:npg]
    v_refs = rest[npg:2 * npg]
    o_ref = rest[2 * npg]
    q4_ref, bt_ref, acc_ref, car_ref = rest[2 * npg + 1:]
    c = pl.program_id(1)
    rows = GROUP * t
    m2 = m2_ref[...]

    @pl.when(c == 0)
    def _():
        _stage_sample_q(q_ref, q4_ref, t)
        row_g = lax.broadcasted_iota(jnp.int32, (rows, PAGE_SIZE), 0) // t
        t_in = lax.broadcasted_iota(jnp.int32, (rows, PAGE_SIZE), 0) % t
        s_in = lax.broadcasted_iota(jnp.int32, (rows, PAGE_SIZE), 1)
        pad = jnp.zeros((PAGE_SIZE - t, KV_DIM), F32)
        kn = jnp.concatenate([kn_ref[...], pad], axis=0).astype(BF16)
        vn = jnp.concatenate([vn_ref[...], pad], axis=0).astype(BF16)
        for kvh in range(N_KV_HEADS):
            bt = jnp.zeros((rows, PAGE_SIZE), F32)
            for g in range(GROUP):
                bt = jnp.where(row_g == g, bias_ref[kvh * GROUP + g], bt)
            bt_ref[kvh] = bt
            pv, inc = _sb_block(q4_ref[kvh], kn[:, _head_cols(kvh)], vn[:, _head_cols(kvh)],
                                bt, jnp.zeros((rows, PAGE_SIZE), F32), m2, s_in < t_in)
            acc_ref[kvh] = pv
            car_ref[kvh] = inc

    ks = [r[...].astype(BF16) for r in k_refs]
    vs = [r[...].astype(BF16) for r in v_refs]
    for kvh in range(N_KV_HEADS):
        q4 = q4_ref[kvh]
        bt = bt_ref[kvh]
        z = jnp.concatenate([_dot_nt(q4, kp[:, _head_cols(kvh)]) + bt for kp in ks], axis=0)
        sp = jnp.maximum(z, 0.0) + jnp.log(1.0 + jnp.exp(-jnp.abs(z)))
        sp_hi = sp.astype(BF16)
        sp_lo = (sp - sp_hi.astype(F32)).astype(BF16)
        res = _dot(jnp.concatenate([sp_hi, sp_lo], axis=1), m2)
        inc = res[:, PAGE_SIZE:]
        carry = car_ref[kvh]
        carries = []
        for i in range(npg):
            carries.append(carry)
            carry = carry + inc[i * rows:(i + 1) * rows]
        car_ref[kvh] = carry
        a = jnp.exp(z - sp - res[:, :PAGE_SIZE] - jnp.concatenate(carries, axis=0)).astype(BF16)
        acc = acc_ref[kvh]
        for i in range(npg):
            acc = acc + _dot(a[i * rows:(i + 1) * rows], vs[i][:, _head_cols(kvh)])
        acc_ref[kvh] = acc

    @pl.when(c == pl.num_programs(1) - 1)
    def _():
        for kvh in range(N_KV_HEADS):
            for g in range(GROUP):
                o_ref[:, _head_cols(kvh * GROUP + g)] = acc_ref[kvh, g * t:(g + 1) * t, :].astype(BF16)


def sb_sample(q_s, k_new, v_new, pool_k, pool_v, page_table, layer, bias):
    db, t, _ = q_s.shape
    n_pages = page_table.shape[1]
    npg = PAGES_PER_STEP
    nc = n_pages // npg
    rows = GROUP * t
    seq_blk = lambda b, c, pt: (b, 0, 0)

    def page_spec(i):
        return pl.BlockSpec(
            (None, None, PAGE_SIZE, KV_DIM),
            lambda b, c, pt: (layer, pt[b * n_pages + (n_pages - 1 - (c * npg + i))], 0, 0))

    grid_spec = pltpu.PrefetchScalarGridSpec(
        num_scalar_prefetch=1,
        grid=(db, nc),
        in_specs=[pl.BlockSpec(memory_space=pltpu.SMEM),
                  pl.BlockSpec((None, t, Q_DIM), seq_blk),
                  pl.BlockSpec((None, t, KV_DIM), seq_blk),
                  pl.BlockSpec((None, t, KV_DIM), seq_blk),
                  pl.BlockSpec((2 * PAGE_SIZE, 2 * PAGE_SIZE), lambda b, c, pt: (0, 0))]
                 + [page_spec(i) for i in range(npg)] * 2,
        out_specs=pl.BlockSpec((None, t, Q_DIM), seq_blk),
        scratch_shapes=[pltpu.VMEM((N_KV_HEADS, rows, HEAD_DIM), BF16),
                        pltpu.VMEM((N_KV_HEADS, rows, PAGE_SIZE), F32),
                        pltpu.VMEM((N_KV_HEADS, rows, HEAD_DIM), F32),
                        pltpu.VMEM((N_KV_HEADS, rows, PAGE_SIZE), F32)])
    return pl.pallas_call(
        functools.partial(_sb_sample_kernel, t=t, npg=npg),
        grid_spec=grid_spec,
        out_shape=jax.ShapeDtypeStruct((db, t, Q_DIM), BF16),
        compiler_params=_cparams(("parallel", "arbitrary")),
        name="sb_sample",
    )(page_table.reshape(-1), bias, q_s, k_new, v_new, _cumsum_matrix(PAGE_SIZE),
      *([pool_k] * npg), *([pool_v] * npg))


def _alibi_slopes():
    h = np.arange(1, N_HEADS + 1, dtype=np.float64)
    return np.exp2(-8.0 * h / N_HEADS).reshape(N_KV_HEADS, GROUP)


def _swa_tile(t_q, dist):
    valid = (dist >= 0) & (dist < WINDOW)
    tile = np.where(valid[None, None], -_alibi_slopes()[:, :, None, None] * dist[None, None], -np.inf)
    return jnp.asarray(tile.reshape(N_KV_HEADS, GROUP * t_q, dist.shape[1]), dtype=F32)


def _swa_core(q4, kband, vband, tile, sink):
    s = _dot_nt(q4, kband) + tile
    m = jnp.maximum(jnp.max(s, axis=1, keepdims=True), sink)
    p = jnp.exp(s - m)
    denom = jnp.sum(p, axis=1, keepdims=True) + jnp.exp(sink - m)
    return _dot((p * (1.0 / denom)).astype(BF16), vband)


def _sink_col(sinks_ref, kvh, rows, t):
    row_g = lax.broadcasted_iota(jnp.int32, (rows, 1), 0) // t
    col = jnp.zeros((rows, 1), F32)
    for g in range(GROUP):
        col = jnp.where(row_g == g, sinks_ref[kvh * GROUP + g], col)
    return col


def _swa_prompt_kernel(sinks_ref, q_ref, kp_ref, kc_ref, vp_ref, vc_ref, tile_ref, o_ref, *, tq):
    qi = pl.program_id(1)
    rows = GROUP * tq
    col = lax.broadcasted_iota(jnp.int32, (rows, 2 * tq), 1)
    keep = (col >= tq) | (qi > 0)
    kband = jnp.concatenate([kp_ref[...], kc_ref[...]], axis=0)
    vband = jnp.concatenate([vp_ref[...], vc_ref[...]], axis=0)
    for kvh in range(N_KV_HEADS):
        q4 = jnp.concatenate([q_ref[:, _head_cols(kvh * GROUP + g)] for g in range(GROUP)], axis=0)
        tile = jnp.where(keep, tile_ref[kvh], NEG_INF)
        o = _swa_core(q4, kband[:, _head_cols(kvh)], vband[:, _head_cols(kvh)], tile,
                      _sink_col(sinks_ref, kvh, rows, tq))
        for g in range(GROUP):
            o_ref[:, _head_cols(kvh * GROUP + g)] = o[g * tq:(g + 1) * tq].astype(BF16)


def swa_prompt(q, kb, vb, sinks, batch, seq):
    tq = WINDOW
    nq = seq // tq
    qi = np.arange(tq)[:, None]
    kj = np.arange(2 * tq)[None, :]
    tile = _swa_tile(tq, tq + qi - kj)
    cur = lambda b, i: (b * nq + i, 0)
    prev = lambda b, i: (b * nq + jnp.maximum(i - 1, 0), 0)
    return pl.pallas_call(
        functools.partial(_swa_prompt_kernel, tq=tq),
        grid=(batch, nq),
        in_specs=[pl.BlockSpec(memory_space=pltpu.SMEM),
                  pl.BlockSpec((tq, Q_DIM), cur),
                  pl.BlockSpec((tq, KV_DIM), prev),
                  pl.BlockSpec((tq, KV_DIM), cur),
                  pl.BlockSpec((tq, KV_DIM), prev),
                  pl.BlockSpec((tq, KV_DIM), cur),
                  pl.BlockSpec((N_KV_HEADS, GROUP * tq, 2 * tq), lambda b, i: (0, 0, 0))],
        out_specs=pl.BlockSpec((tq, Q_DIM), cur),
        out_shape=jax.ShapeDtypeStruct((batch * seq, Q_DIM), BF16),
        compiler_params=_cparams(("parallel", "parallel")),
        name="swa_prompt",
    )(sinks, q, kb, kb, vb, vb, tile)


def _swa_sample_kernel(sinks_ref, q_ref, kn_ref, vn_ref, kbuf_ref, vbuf_ref, tile_ref,
                       o_ref, ko_ref, vo_ref, *, t, w_buf):
    rows = GROUP * t
    pad = jnp.zeros((w_buf - t, KV_DIM), F32)
    sink_cols = [_sink_col(sinks_ref, kvh, rows, t) for kvh in range(N_KV_HEADS)]
    for s in range(q_ref.shape[0]):
        kn = kn_ref[s]
        vn = vn_ref[s]
        kbuf = kbuf_ref[s]
        vbuf = vbuf_ref[s]
        ko_ref[s] = jnp.concatenate([kbuf[t:], kn], axis=0)
        vo_ref[s] = jnp.concatenate([vbuf[t:], vn], axis=0)
        kband = jnp.concatenate([kbuf, kn, pad], axis=0).astype(BF16)
        vband = jnp.concatenate([vbuf, vn, pad], axis=0).astype(BF16)
        q = q_ref[s].astype(F32)
        for kvh in range(N_KV_HEADS):
            q4 = jnp.concatenate(
                [q[:, _head_cols(kvh * GROUP + g)] for g in range(GROUP)], axis=0).astype(BF16)
            o = _swa_core(q4, kband[:, _head_cols(kvh)], vband[:, _head_cols(kvh)],
                          tile_ref[kvh], sink_cols[kvh])
            for g in range(GROUP):
                o_ref[s, :, _head_cols(kvh * GROUP + g)] = o[g * t:(g + 1) * t].astype(BF16)


def swa_sample(q_s, k_new, v_new, buf_k, buf_v, layer, sinks):
    db, t, _ = q_s.shape
    w_buf = buf_k.shape[2]
    ns = SWA_SEQS_PER_STEP
    idx = np.arange(2 * w_buf)[None, :]
    dist = np.where(idx < w_buf + t, (w_buf + np.arange(t))[:, None] - idx, -1)
    tile = _swa_tile(t, dist)
    seq_blk = lambda i: (i, 0, 0)
    buf_blk = lambda i: (layer, i, 0, 0)
    return pl.pallas_call(
        functools.partial(_swa_sample_kernel, t=t, w_buf=w_buf),
        grid=(db // ns,),
        in_specs=[pl.BlockSpec(memory_space=pltpu.SMEM),
                  pl.BlockSpec((ns, t, Q_DIM), seq_blk),
                  pl.BlockSpec((ns, t, KV_DIM), seq_blk),
                  pl.BlockSpec((ns, t, KV_DIM), seq_blk),
                  pl.BlockSpec((None, ns, w_buf, KV_DIM), buf_blk),
                  pl.BlockSpec((None, ns, w_buf, KV_DIM), buf_blk),
                  pl.BlockSpec((N_KV_HEADS, GROUP * t, 2 * w_buf), lambda i: (0, 0, 0))],
        out_specs=[pl.BlockSpec((ns, t, Q_DIM), seq_blk),
                   pl.BlockSpec((ns, w_buf, KV_DIM), seq_blk),
                   pl.BlockSpec((ns, w_buf, KV_DIM), seq_blk)],
        out_shape=[jax.ShapeDtypeStruct((db, t, Q_DIM), BF16),
                   jax.ShapeDtypeStruct((db, w_buf, KV_DIM), F32),
                   jax.ShapeDtypeStruct((db, w_buf, KV_DIM), F32)],
        compiler_params=_cparams(("parallel",)),
        name="swa_sample",
    )(sinks, q_s, k_new, v_new, buf_k, buf_v, tile)


def _pad_router(w):
    w = jnp.pad(w, ((0, 0), (0, ROUTER_LANES - w.shape[1])))
    hi = w.astype(BF16)
    return hi, (w - hi.astype(F32)).astype(BF16)


def _token_tile(n):
    for tm in (512, 256, 128, 64, 32, 16, 8):
        if n % tm == 0:
            return tm
    raise ValueError(f"token count {n} is not a multiple of 8")


def kernel(x_prompt, x_sample, cache_sb_k, cache_sb_v, cache_swa_k, cache_swa_v, page_table,
           p_prompt, p_sample, attn_norm, w_qkv, w_o, sb_bias, attn_sinks, ffn_norm, w_gu_dense,
           w_down_dense, w_router, w_gu_expert, w_down_expert, ple_norm, w_ple_proj,
           w_ple_gate, final_norm):
    batch, seq, _ = x_prompt.shape
    db, t, _ = x_sample.shape
    depth = attn_norm.shape[0]
    n_p = batch * seq
    n_s = db * t
    n = n_p + n_s
    tm = _token_tile(n)
    assert n_p % tm == 0 and seq % WINDOW == 0

    h = jnp.concatenate([x_prompt.reshape(n_p, D_MODEL), x_sample.reshape(n_s, D_MODEL)], axis=0)
    p_all = jnp.concatenate([p_prompt.reshape(depth, n_p, -1), p_sample.reshape(depth, n_s, -1)], axis=1)
    pool_k = cache_sb_k.reshape(*cache_sb_k.shape[:3], KV_DIM)
    pool_v = cache_sb_v.reshape(*cache_sb_v.shape[:3], KV_DIM)
    buf_k = cache_swa_k.reshape(*cache_swa_k.shape[:3], KV_DIM)
    buf_v = cache_swa_v.reshape(*cache_swa_v.shape[:3], KV_DIM)
    w_buf = buf_k.shape[2]
    w_keep = min(WINDOW, seq)
    d_ff = w_down_dense.shape[1]
    d_ffe = w_down_expert.shape[2]

    def kv_heads(x, lead):
        return x.reshape(*lead, N_KV_HEADS, HEAD_DIM)

    sb_kp, sb_vp, sb_ks, sb_vs = [], [], [], []
    sw_kp, sw_vp, sw_ks, sw_vs = [], [], [], []
    for i in range(depth):
        j = i // 2
        q, k, v, kb, vb = qkv_proj(h, attn_norm[i][None], w_qkv[i].astype(BF16), tm)
        q_s = q[n_p:].reshape(db, t, Q_DIM)
        k_s = k[n_p:].reshape(db, t, KV_DIM)
        v_s = v[n_p:].reshape(db, t, KV_DIM)
        if i % 2 == 0:
            o_p = sb_prompt(q, kb, vb, sb_bias[j], batch, seq, PAGE_SIZE)
            o_s = sb_sample(q_s, k_s, v_s, pool_k, pool_v, page_table, j, sb_bias[j])
            sb_kp.append(kv_heads(k[:n_p], (batch, seq)))
            sb_vp.append(kv_heads(v[:n_p], (batch, seq)))
            sb_ks.append(kv_heads(k_s, (db, t)))
            sb_vs.append(kv_heads(v_s, (db, t)))
        else:
            o_p = swa_prompt(q, kb, vb, attn_sinks[j], batch, seq)
            o_s, bks, bvs = swa_sample(q_s, k_s, v_s, buf_k, buf_v, j, attn_sinks[j])
            sw_kp.append(kv_heads(k[:n_p].reshape(batch, seq, KV_DIM)[:, seq - w_keep:], (batch, w_keep)))
            sw_vp.append(kv_heads(v[:n_p].reshape(batch, seq, KV_DIM)[:, seq - w_keep:], (batch, w_keep)))
            sw_ks.append(kv_heads(bks, (db, w_buf)))
            sw_vs.append(kv_heads(bvs, (db, w_buf)))
        attn = jnp.concatenate([o_p, o_s.reshape(n_s, Q_DIM)], axis=0)
        if i % 2 == 0:
            h1, xn = post_attn(h, attn, w_o[i].astype(BF16), ffn_norm[i][None], None, tm)
            h2 = dense_ffn(h1, xn, w_gu_dense[j].astype(BF16), w_down_dense[j].astype(BF16), tm, d_ff // 2)
        else:
            h1, xn, gates = post_attn(h, attn, w_o[i].astype(BF16), ffn_norm[i][None],
                                      _pad_router(w_router[j]), tm)
            h2 = moe_dense(h1, xn, gates, w_gu_expert[j].astype(BF16), w_down_expert[j].astype(BF16),
                           tm, d_ffe // 2)
        h = ple_add(h2, p_all[i], ple_norm[i][None], w_ple_proj[i].astype(BF16),
                    w_ple_gate[i].astype(BF16), final_norm[None] if i == depth - 1 else None, tm)

    y_prompt = h[:n_p].reshape(batch, seq, D_MODEL)
    y_sample = h[n_p:].reshape(db, t, D_MODEL)
    return (y_prompt, y_sample,
            jnp.stack(sb_kp), jnp.stack(sb_vp), jnp.stack(sb_ks), jnp.stack(sb_vs),
            jnp.stack(sw_kp), jnp.stack(sw_vp), jnp.stack(sw_ks), jnp.stack(sw_vs))
```

```python
import functools

import numpy as np
import jax
import jax.numpy as jnp
from jax import lax
from jax.experimental import pallas as pl
from jax.experimental.pallas import tpu as pltpu

F32 = jnp.float32
BF16 = jnp.bfloat16

D_MODEL = 1024
N_HEADS = 16
HEAD_DIM = 64
N_KV_HEADS = 4
GROUP = N_HEADS // N_KV_HEADS
Q_DIM = N_HEADS * HEAD_DIM
KV_DIM = N_KV_HEADS * HEAD_DIM
QKV_DIM = Q_DIM + 2 * KV_DIM
ATTN_SCALE = HEAD_DIM ** -0.5
WINDOW = 128
PAGE_SIZE = 128
N_EXPERTS = 8
TOP_K = 2
RMS_EPS = 1e-6

LANES = 128
ROUTER_LANES = LANES
VMEM_LIMIT = 56 * 1024 * 1024
PAGES_PER_STEP = 8
SWA_SEQS_PER_STEP = 8
SB_KEY_BLOCK = 256
DMA_LOOP_UNROLL = 8
NEG_INF = float("-inf")
BIAS_PARTS = 3
LANE_EXPERT = N_EXPERTS
LANE_GATE = N_EXPERTS + TOP_K


def _cparams(sem, vmem=VMEM_LIMIT):
    return pltpu.CompilerParams(dimension_semantics=sem, vmem_limit_bytes=vmem)


def _dot(a, b):
    return jnp.dot(a, b, preferred_element_type=F32)


def _dot_nt(a, b):
    return lax.dot_general(a, b, (((1,), (1,)), ((), ())), preferred_element_type=F32)


def _head_cols(h):
    return slice(h * HEAD_DIM, (h + 1) * HEAD_DIM)


def _rms(x, g):
    ms = jnp.mean(x * x, axis=-1, keepdims=True)
    return x * lax.rsqrt(ms + RMS_EPS) * g


def _neg_abs(z):
    return lax.bitcast_convert_type(lax.bitcast_convert_type(z, jnp.int32) | jnp.int32(-2 ** 31), F32)


def _softplus(z):
    return jnp.maximum(z, 0.0) + jnp.log(1.0 + jnp.exp(_neg_abs(z)))


def _qkv(h_ref, g_ref, w_ref, q_ref):
    xn = _rms(h_ref[...], g_ref[...]).astype(BF16)
    t = _dot(xn, w_ref[...])
    q_ref[...] = (t[:, :Q_DIM] * ATTN_SCALE).astype(BF16)
    return t[:, Q_DIM:Q_DIM + KV_DIM], t[:, Q_DIM + KV_DIM:]


def _qkv_prompt_kernel(h_ref, g_ref, w_ref, q_ref, kt_ref, vt_ref, kb_ref, vb_ref):
    k, v = _qkv(h_ref, g_ref, w_ref, q_ref)
    kt = k.T
    vt = v.T
    row = lax.broadcasted_iota(jnp.int32, (LANES - HEAD_DIM, k.shape[0]), 0)
    ones = jnp.where(row < BIAS_PARTS, 1.0, 0.0).astype(BF16)
    for kvh in range(N_KV_HEADS):
        kt_ref[kvh] = kt[_head_cols(kvh)]
        vt_ref[kvh] = vt[_head_cols(kvh)]
        kb_ref[kvh, :HEAD_DIM] = kt[_head_cols(kvh)].astype(BF16)
        kb_ref[kvh, HEAD_DIM:] = ones
        vb_ref[:, kvh * LANES:kvh * LANES + HEAD_DIM] = v[:, _head_cols(kvh)].astype(BF16)
        vb_ref[:, kvh * LANES + HEAD_DIM:(kvh + 1) * LANES] = jnp.zeros((v.shape[0], LANES - HEAD_DIM), BF16)


def _qkv_sample_kernel(h_ref, g_ref, w_ref, q_ref, k_ref, v_ref):
    k, v = _qkv(h_ref, g_ref, w_ref, q_ref)
    for kvh in range(N_KV_HEADS):
        k_ref[:, kvh, :] = k[:, _head_cols(kvh)]
        v_ref[:, kvh, :] = v[:, _head_cols(kvh)]


def _qkv_in_specs(tm, row):
    return [pl.BlockSpec((tm, D_MODEL), row),
            pl.BlockSpec((1, D_MODEL), lambda *_: (0, 0)),
            pl.BlockSpec((D_MODEL, QKV_DIM), lambda *_: (0, 0))]


def qkv_prompt(h, g, w, batch, seq, tm):
    ns = seq // tm
    row = lambda b, i: (b * ns + i, 0)
    tblk = lambda b, i: (b, 0, 0, i)
    return pl.pallas_call(
        _qkv_prompt_kernel,
        grid=(batch, ns),
        in_specs=_qkv_in_specs(tm, row),
        out_specs=[pl.BlockSpec((tm, Q_DIM), row),
                   pl.BlockSpec((None, N_KV_HEADS, HEAD_DIM, tm), tblk),
                   pl.BlockSpec((None, N_KV_HEADS, HEAD_DIM, tm), tblk),
                   pl.BlockSpec((None, N_KV_HEADS, LANES, tm), tblk),
                   pl.BlockSpec((tm, N_KV_HEADS * LANES), row)],
        out_shape=[jax.ShapeDtypeStruct((batch * seq, Q_DIM), BF16),
                   jax.ShapeDtypeStruct((batch, N_KV_HEADS, HEAD_DIM, seq), F32),
                   jax.ShapeDtypeStruct((batch, N_KV_HEADS, HEAD_DIM, seq), F32),
                   jax.ShapeDtypeStruct((batch, N_KV_HEADS, LANES, seq), BF16),
                   jax.ShapeDtypeStruct((batch * seq, N_KV_HEADS * LANES), BF16)],
        compiler_params=_cparams(("parallel", "parallel")),
        name="qkv_prompt",
    )(h, g, w)


def qkv_sample(h, g, w, first_row, n_rows, tm):
    first = first_row // tm
    row_in = lambda i: (first + i, 0)
    row = lambda i: (i, 0)
    row3 = lambda i: (i, 0, 0)
    return pl.pallas_call(
        _qkv_sample_kernel,
        grid=(n_rows // tm,),
        in_specs=_qkv_in_specs(tm, row_in),
        out_specs=[pl.BlockSpec((tm, Q_DIM), row),
                   pl.BlockSpec((tm, N_KV_HEADS, HEAD_DIM), row3),
                   pl.BlockSpec((tm, N_KV_HEADS, HEAD_DIM), row3)],
        out_shape=[jax.ShapeDtypeStruct((n_rows, Q_DIM), BF16),
                   jax.ShapeDtypeStruct((n_rows, N_KV_HEADS, HEAD_DIM), F32),
                   jax.ShapeDtypeStruct((n_rows, N_KV_HEADS, HEAD_DIM), F32)],
        compiler_params=_cparams(("parallel",)),
        name="qkv_sample",
    )(h, g, w)


def _route(xn, wh_ref, wl_ref):
    xh = xn.astype(BF16)
    xl = (xn - xh.astype(F32)).astype(BF16)
    wh = wh_ref[...]
    logits = _dot(xh, wh) + _dot(xl, wh) + _dot(xh, wl_ref[...])
    lane = lax.broadcasted_iota(jnp.int32, logits.shape, 1)
    logits = jnp.where(lane < N_EXPERTS, logits, NEG_INF)
    m1 = jnp.max(logits, axis=1, keepdims=True)
    i1 = jnp.min(jnp.where(logits == m1, lane, ROUTER_LANES), axis=1, keepdims=True)
    rest = jnp.where(lane == i1, NEG_INF, logits)
    m2 = jnp.max(rest, axis=1, keepdims=True)
    i2 = jnp.min(jnp.where(rest == m2, lane, ROUTER_LANES), axis=1, keepdims=True)
    e2 = jnp.exp(m2 - m1)
    g1 = 1.0 / (1.0 + e2)
    out = jnp.where(lane == LANE_EXPERT, i1.astype(F32), 0.0)
    out = jnp.where(lane == LANE_EXPERT + 1, i2.astype(F32), out)
    out = jnp.where(lane == LANE_GATE, g1, out)
    return jnp.where(lane == LANE_GATE + 1, e2 * g1, out)


def _post_attn_kernel(h_ref, ap_ref, as_ref, wo_ref, g_ref, *rest, routed, prompt_tiles):
    if routed:
        wh_ref, wl_ref, h1_ref, xn_ref, route_ref = rest
    else:
        h1_ref, xn_ref = rest
    attn = jnp.where(pl.program_id(0) < prompt_tiles, ap_ref[...], as_ref[...])
    h1 = h_ref[...] + _dot(attn, wo_ref[...])
    h1_ref[...] = h1
    xn = _rms(h1, g_ref[...])
    xn_ref[...] = xn.astype(xn_ref.dtype)
    if routed:
        route_ref[...] = _route(xn, wh_ref, wl_ref)


def post_attn(h, attn_p, attn_s, wo, g, router, tm):
    n = h.shape[0]
    prompt_tiles = attn_p.shape[0] // tm
    sample_tiles = attn_s.shape[0] // tm
    row = lambda i: (i, 0)
    fixed = lambda i: (0, 0)
    routed = router is not None
    in_specs = [pl.BlockSpec((tm, D_MODEL), row),
                pl.BlockSpec((tm, Q_DIM), lambda i: (jnp.minimum(i, prompt_tiles - 1), 0)),
                pl.BlockSpec((tm, Q_DIM), lambda i: (jnp.clip(i - prompt_tiles, 0, sample_tiles - 1), 0)),
                pl.BlockSpec((Q_DIM, D_MODEL), fixed),
                pl.BlockSpec((1, D_MODEL), fixed)]
    out_specs = [pl.BlockSpec((tm, D_MODEL), row), pl.BlockSpec((tm, D_MODEL), row)]
    out_shape = [jax.ShapeDtypeStruct((n, D_MODEL), F32),
                 jax.ShapeDtypeStruct((n, D_MODEL), F32 if routed else BF16)]
    args = [h, attn_p, attn_s, wo, g]
    if routed:
        in_specs += [pl.BlockSpec((D_MODEL, ROUTER_LANES), fixed)] * 2
        out_specs.append(pl.BlockSpec((tm, ROUTER_LANES), row))
        out_shape.append(jax.ShapeDtypeStruct((n, ROUTER_LANES), F32))
        args += list(router)
    return pl.pallas_call(
        functools.partial(_post_attn_kernel, routed=routed, prompt_tiles=prompt_tiles),
        grid=(n // tm,),
        in_specs=in_specs, out_specs=out_specs, out_shape=out_shape,
        compiler_params=_cparams(("parallel",)),
        name="post_attn_routed" if routed else "post_attn",
    )(*args)


def _swiglu_partial(x, wg, wu, wd):
    g = _dot(x, wg)
    u = _dot(x, wu)
    return _dot((g * jax.nn.sigmoid(g) * u).astype(BF16), wd)


def _ffn_kernel(h_ref, x_ref, wg_ref, wu_ref, wd_ref, o_ref, acc_ref):
    f = pl.program_id(1)

    @pl.when(f == 0)
    def _():
        acc_ref[...] = jnp.zeros_like(acc_ref)

    acc_ref[...] += _swiglu_partial(x_ref[...], wg_ref[...], wu_ref[...], wd_ref[...])

    @pl.when(f == pl.num_programs(1) - 1)
    def _():
        o_ref[...] = h_ref[...] + acc_ref[...]


def dense_ffn(h, xn, w_gu, w_down, tm, tf):
    n = h.shape[0]
    d_ff = w_down.shape[0]
    nf = d_ff // tf
    row = lambda i, f: (i, 0)
    return pl.pallas_call(
        _ffn_kernel,
        grid=(n // tm, nf),
        in_specs=[pl.BlockSpec((tm, D_MODEL), row),
                  pl.BlockSpec((tm, D_MODEL), row),
                  pl.BlockSpec((D_MODEL, tf), lambda i, f: (0, f)),
                  pl.BlockSpec((D_MODEL, tf), lambda i, f: (0, nf + f)),
                  pl.BlockSpec((tf, D_MODEL), lambda i, f: (f, 0))],
        out_specs=pl.BlockSpec((tm, D_MODEL), row),
        out_shape=jax.ShapeDtypeStruct((n, D_MODEL), F32),
        scratch_shapes=[pltpu.VMEM((tm, D_MODEL), F32)],
        compiler_params=_cparams(("parallel", "arbitrary")),
        name="dense_ffn",
    )(h, xn, w_gu, w_gu, w_down)


def _ple_kernel(h_ref, p_ref, g_ref, wp_ref, wg_ref, *rest, final):
    if final:
        gf_ref, o_ref = rest
    else:
        (o_ref,) = rest
    h = h_ref[...]
    gate = jax.nn.sigmoid(_dot(_rms(h, g_ref[...]).astype(BF16), wg_ref[...]))
    out = h + _dot(p_ref[...].astype(BF16), wp_ref[...]) * gate
    if final:
        out = _rms(out, gf_ref[...])
    o_ref[...] = out


def ple_add(h, p, g, w_proj, w_gate, final_g, tm):
    n = h.shape[0]
    ple_dim = p.shape[1]
    row = lambda i: (i, 0)
    fixed = lambda i: (0, 0)
    final = final_g is not None
    in_specs = [pl.BlockSpec((tm, D_MODEL), row),
                pl.BlockSpec((tm, ple_dim), row),
                pl.BlockSpec((1, D_MODEL), fixed),
                pl.BlockSpec((ple_dim, D_MODEL), fixed),
                pl.BlockSpec((D_MODEL, D_MODEL), fixed)]
    args = [h, p, g, w_proj, w_gate]
    if final:
        in_specs.append(pl.BlockSpec((1, D_MODEL), fixed))
        args.append(final_g)
    return pl.pallas_call(
        functools.partial(_ple_kernel, final=final),
        grid=(n // tm,),
        in_specs=in_specs,
        out_specs=pl.BlockSpec((tm, D_MODEL), row),
        out_shape=jax.ShapeDtypeStruct((n, D_MODEL), F32),
        compiler_params=_cparams(("parallel",)),
        name="ple_final" if final else "ple",
    )(*args)


def _row_copy(src_ref, src_row, dst_ref, dst_row, sem):
    return pltpu.make_async_copy(src_ref.at[pl.ds(src_row, 1)], dst_ref.at[pl.ds(dst_row, 1)], sem)


def _moe_dispatch_kernel(pos_ref, x_ref, xs_in_ref, xs_ref, sem, *, tm):
    del xs_in_ref

    def copies(r):
        return [_row_copy(x_ref, r, xs_ref, pos_ref[s, r], sem) for s in range(TOP_K)]

    def start(r, c):
        for cp in copies(r):
            cp.start()
        return c

    def wait(r, c):
        for cp in copies(r):
            cp.wait()
        return c

    lax.fori_loop(0, tm, start, 0, unroll=DMA_LOOP_UNROLL)
    lax.fori_loop(0, tm, wait, 0, unroll=DMA_LOOP_UNROLL)


def moe_dispatch(xn, pos_tiles, n_rows, tm):
    n = xn.shape[0]
    xs0 = jnp.zeros((n_rows, D_MODEL), F32)
    return pl.pallas_call(
        functools.partial(_moe_dispatch_kernel, tm=tm),
        grid=(n // tm,),
        in_specs=[pl.BlockSpec((None, TOP_K, tm), lambda i: (i, 0, 0), memory_space=pltpu.SMEM),
                  pl.BlockSpec((tm, D_MODEL), lambda i: (i, 0)),
                  pl.BlockSpec(memory_space=pl.ANY)],
        out_specs=pl.BlockSpec(memory_space=pl.ANY),
        out_shape=jax.ShapeDtypeStruct((n_rows, D_MODEL), F32),
        scratch_shapes=[pltpu.SemaphoreType.DMA(())],
        input_output_aliases={2: 0},
        compiler_params=_cparams(("arbitrary",)),
        name="moe_dispatch",
    )(pos_tiles, xn, xs0)


def _moe_grouped_kernel(te_ref, nt_ref, x_ref, wg_ref, wu_ref, wd_ref, o_ref, acc_ref):
    i = pl.program_id(0)
    f = pl.program_id(1)
    live = i < nt_ref[0]

    @pl.when(live & (f == 0))
    def _():
        acc_ref[...] = jnp.zeros_like(acc_ref)

    @pl.when(live)
    def _():
        acc_ref[...] += _swiglu_partial(x_ref[...].astype(BF16), wg_ref[0], wu_ref[0], wd_ref[0])

    @pl.when(live & (f == pl.num_programs(1) - 1))
    def _():
        o_ref[...] = acc_ref[...]

    @pl.when(jnp.logical_not(live) & (f == pl.num_programs(1) - 1))
    def _():
        o_ref[...] = jnp.zeros_like(o_ref)


def moe_grouped(xs, tile_expert, n_tiles, w_gu, w_down, tm, tf):
    n_rows = xs.shape[0]
    max_tiles = n_rows // tm
    d_ff = w_down.shape[1]
    nf = d_ff // tf
    last = lambda i, nt: jnp.minimum(i, nt[0] - 1)
    fblk = lambda i, f, nt: jnp.where(i < nt[0], f, nf - 1)
    row = lambda i, f, te, nt: (last(i, nt), 0)
    grid_spec = pltpu.PrefetchScalarGridSpec(
        num_scalar_prefetch=2,
        grid=(max_tiles, nf),
        in_specs=[pl.BlockSpec((tm, D_MODEL), row),
                  pl.BlockSpec((1, D_MODEL, tf), lambda i, f, te, nt: (te[last(i, nt)], 0, fblk(i, f, nt))),
                  pl.BlockSpec((1, D_MODEL, tf), lambda i, f, te, nt: (te[last(i, nt)], 0, nf + fblk(i, f, nt))),
                  pl.BlockSpec((1, tf, D_MODEL), lambda i, f, te, nt: (te[last(i, nt)], fblk(i, f, nt), 0))],
        out_specs=pl.BlockSpec((tm, D_MODEL), lambda i, f, te, nt: (i, 0)),
        scratch_shapes=[pltpu.VMEM((tm, D_MODEL), F32)])
    return pl.pallas_call(
        _moe_grouped_kernel,
        grid_spec=grid_spec,
        out_shape=jax.ShapeDtypeStruct((n_rows, D_MODEL), F32),
        compiler_params=_cparams(("arbitrary", "arbitrary")),
        name="moe_grouped",
    )(tile_expert, n_tiles, xs, w_gu, w_gu, w_down)


def _moe_combine_kernel(pos_ref, h_ref, route_ref, ys_ref, o_ref, ybuf_ref, sem, *, tm):
    def copies(r):
        return [_row_copy(ys_ref, pos_ref[s, r], ybuf_ref.at[s], r, sem) for s in range(TOP_K)]

    def start(r, c):
        for cp in copies(r):
            cp.start()
        return c

    def wait(r, c):
        for cp in copies(r):
            cp.wait()
        return c

    lax.fori_loop(0, tm, start, 0, unroll=DMA_LOOP_UNROLL)
    route = route_ref[...]
    lane = lax.broadcasted_iota(jnp.int32, route.shape, 1)
    gates = [jnp.sum(jnp.where(lane == LANE_GATE + s, route, 0.0), axis=1, keepdims=True)
             for s in range(TOP_K)]
    lax.fori_loop(0, tm, wait, 0, unroll=DMA_LOOP_UNROLL)
    out = gates[0] * ybuf_ref[0]
    for s in range(1, TOP_K):
        out = out + gates[s] * ybuf_ref[s]
    o_ref[...] = h_ref[...] + out


def moe_combine(h, route, ys, pos_tiles, tm):
    n = h.shape[0]
    row = lambda i: (i, 0)
    return pl.pallas_call(
        functools.partial(_moe_combine_kernel, tm=tm),
        grid=(n // tm,),
        in_specs=[pl.BlockSpec((None, TOP_K, tm), lambda i: (i, 0, 0), memory_space=pltpu.SMEM),
                  pl.BlockSpec((tm, D_MODEL), row),
                  pl.BlockSpec((tm, ROUTER_LANES), row),
                  pl.BlockSpec(memory_space=pl.ANY)],
        out_specs=pl.BlockSpec((tm, D_MODEL), row),
        out_shape=jax.ShapeDtypeStruct((n, D_MODEL), F32),
        scratch_shapes=[pltpu.VMEM((TOP_K, tm, D_MODEL), F32), pltpu.SemaphoreType.DMA(())],
        compiler_params=_cparams(("arbitrary",)),
        name="moe_combine",
    )(pos_tiles, h, route, ys)


def _routing_tables(route, tm):
    n = route.shape[0]
    max_tiles = -(-(TOP_K * n + N_EXPERTS * (tm - 1)) // tm)
    expert = jnp.concatenate([route[:, LANE_EXPERT + s] for s in range(TOP_K)]).astype(jnp.int32)
    onehot = (expert[:, None] == jnp.arange(N_EXPERTS, dtype=jnp.int32)[None, :]).astype(jnp.int32)
    csum = jnp.cumsum(onehot, axis=0)
    counts = csum[-1]
    tiles = (counts + tm - 1) // tm
    tile_end = jnp.cumsum(tiles)
    row_start = (tile_end - tiles) * tm
    pos = jnp.sum(onehot * (csum - 1 + row_start[None, :]), axis=1)
    pos_tiles = pos.reshape(TOP_K, n // tm, tm).transpose(1, 0, 2)
    tile_expert = jnp.sum(jnp.arange(max_tiles, dtype=jnp.int32)[:, None] >= tile_end[None, :], axis=1)
    tile_expert = jnp.minimum(tile_expert, N_EXPERTS - 1).astype(jnp.int32)
    return pos_tiles, tile_expert, tile_end[-1:].astype(jnp.int32), max_tiles * tm


def moe_ffn(h, xn, route, w_gu, w_down, tm, tf):
    pos_tiles, tile_expert, n_tiles, n_rows = _routing_tables(route, tm)
    xs = moe_dispatch(xn, pos_tiles, n_rows, tm)
    ys = moe_grouped(xs, tile_expert, n_tiles, w_gu, w_down, tm, tf)
    return moe_combine(h, route, ys, pos_tiles, tm)


def _later_keys_matrix(tk, with_row_sum):
    j = np.arange(tk)[:, None]
    s = np.arange(2 * tk if with_row_sum else tk)[None, :]
    return jnp.asarray(np.where(s < tk, j > s, True), dtype=BF16)


def _split_bias(bias):
    parts = []
    rest = bias.astype(F32)
    for _ in range(BIAS_PARTS):
        p = rest.astype(BF16).astype(F32)
        parts.append(p)
        rest = rest - p
    return jnp.stack(parts)


def _sb_prompt_kernel(bias_ref, q_ref, k_ref, v_ref, m_ref, o_ref, q4_ref, acc_ref, car_ref, *, tq, tk):
    qi = pl.program_id(1)
    rows = GROUP * tq
    row_g = lax.broadcasted_iota(jnp.int32, (rows, HEAD_DIM), 0) // tq
    lane = lax.broadcasted_iota(jnp.int32, (rows, HEAD_DIM), 1)
    for kvh in range(N_KV_HEADS):
        q4 = jnp.concatenate([q_ref[:, _head_cols(kvh * GROUP + g)] for g in range(GROUP)], axis=0)
        bcols = jnp.zeros((rows, HEAD_DIM), F32)
        for g in range(GROUP):
            for part in range(BIAS_PARTS):
                bcols = jnp.where((row_g == g) & (lane == part), bias_ref[part, kvh * GROUP + g], bcols)
        q4_ref[kvh] = jnp.concatenate([q4, bcols.astype(BF16)], axis=1)
    acc_ref[...] = jnp.zeros_like(acc_ref)
    car_ref[...] = jnp.zeros_like(car_ref)

    def block(j, mask):
        start = pl.multiple_of(j * tk, tk)
        later = m_ref[...]
        heads = range(N_KV_HEADS)
        zs = [_dot(q4_ref[kvh], k_ref[kvh, :, pl.ds(start, tk)]) for kvh in heads]
        sps = [_softplus(z) for z in zs]
        lss = [z - sp for z, sp in zip(zs, sps)]
        if mask is not None:
            sps = [jnp.where(mask, sp, 0.0) for sp in sps]
        sp16 = [sp.astype(BF16) for sp in sps]
        tails = [_dot(s16, later) for s16 in sp16]
        for kvh in heads:
            a = jnp.exp(lss[kvh] - tails[kvh] - car_ref[kvh])
            if mask is not None:
                a = jnp.where(mask, a, 0.0)
            acc_ref[kvh] += _dot(a.astype(BF16), v_ref[pl.ds(start, tk), kvh * LANES:(kvh + 1) * LANES])
            car_ref[kvh] += tails[kvh][:, :1] + sp16[kvh][:, :1].astype(F32)

    jd = (qi * tq) // tk
    t_pos = qi * tq + lax.broadcasted_iota(jnp.int32, (rows, tk), 0) % tq
    s_pos = jd * tk + lax.broadcasted_iota(jnp.int32, (rows, tk), 1)
    block(jd, s_pos < t_pos)

    def body(i, c):
        block(jd - 1 - i, None)
        return c

    lax.fori_loop(0, jd, body, 0)

    for kvh in range(N_KV_HEADS):
        for g in range(GROUP):
            o_ref[:, _head_cols(kvh * GROUP + g)] = acc_ref[kvh, g * tq:(g + 1) * tq, :HEAD_DIM].astype(BF16)


def sb_prompt(q, kb, vb, bias, batch, seq, tq, tk):
    nq = seq // tq
    rows = GROUP * tq
    return pl.pallas_call(
        functools.partial(_sb_prompt_kernel, tq=tq, tk=tk),
        grid=(batch, nq),
        in_specs=[pl.BlockSpec(memory_space=pltpu.SMEM),
                  pl.BlockSpec((tq, Q_DIM), lambda b, i: (b * nq + i, 0)),
                  pl.BlockSpec((None, N_KV_HEADS, LANES, seq), lambda b, i: (b, 0, 0, 0)),
                  pl.BlockSpec((seq, N_KV_HEADS * LANES), lambda b, i: (b, 0)),
                  pl.BlockSpec((tk, tk), lambda b, i: (0, 0))],
        out_specs=pl.BlockSpec((tq, Q_DIM), lambda b, i: (b * nq + i, 0)),
        out_shape=jax.ShapeDtypeStruct((batch * seq, Q_DIM), BF16),
        scratch_shapes=[pltpu.VMEM((N_KV_HEADS, rows, LANES), BF16),
                        pltpu.VMEM((N_KV_HEADS, rows, LANES), F32),
                        pltpu.VMEM((N_KV_HEADS, rows, 1), F32)],
        compiler_params=_cparams(("parallel", "parallel")),
        name="sb_prompt",
    )(_split_bias(bias), q, kb, vb, _later_keys_matrix(tk, False))


def _sb_sample_kernel(pt_ref, bias_ref, q_ref, kn_ref, vn_ref, m_ref, *rest, t, npg):
    k_refs = rest[:npg]
    v_refs = rest[npg:2 * npg]
    o_ref = rest[2 * npg]
    q4_ref, bt_ref, acc_ref, car_ref = rest[2 * npg + 1:]
    c = pl.program_id(1)
    rows = GROUP * t
    later = m_ref[...]
    heads = range(N_KV_HEADS)

    def attend(z, v, pv_dot, carry0, mask, n):
        sp = _softplus(z)
        if mask is not None:
            sp = jnp.where(mask, sp, 0.0)
        res = _dot(sp.astype(BF16), later)
        inc = res[:, PAGE_SIZE:]
        carry = carry0
        carries = []
        for i in range(n):
            carries.append(carry)
            carry = carry + inc[i * rows:(i + 1) * rows]
        a = jnp.exp(z - sp - res[:, :PAGE_SIZE] - jnp.concatenate(carries, axis=0))
        if mask is not None:
            a = jnp.where(mask, a, 0.0)
        a = a.astype(BF16)
        return pv_dot(a, v), carry

    @pl.when(c == 0)
    def _():
        q = q_ref[...].astype(F32)
        row_g = lax.broadcasted_iota(jnp.int32, (rows, PAGE_SIZE), 0) // t
        t_in = lax.broadcasted_iota(jnp.int32, (rows, PAGE_SIZE), 0) % t
        s_in = lax.broadcasted_iota(jnp.int32, (rows, PAGE_SIZE), 1)
        pad = jnp.zeros((PAGE_SIZE - t, HEAD_DIM), F32)
        for kvh in heads:
            q4 = jnp.concatenate([q[:, _head_cols(kvh * GROUP + g)] for g in range(GROUP)], axis=0).astype(BF16)
            q4_ref[kvh] = q4
            bt = jnp.zeros((rows, PAGE_SIZE), F32)
            for g in range(GROUP):
                bt = jnp.where(row_g == g, bias_ref[kvh * GROUP + g], bt)
            bt_ref[kvh] = bt
            kn = jnp.concatenate([kn_ref[:, kvh, :], pad], axis=0).astype(BF16)
            vn = jnp.concatenate([vn_ref[:, kvh, :], pad], axis=0).astype(BF16)
            pv, carry = attend(_dot_nt(q4, kn) + bt, vn, _dot, jnp.zeros((rows, PAGE_SIZE), F32),
                               s_in < t_in, 1)
            acc_ref[kvh] = pv
            car_ref[kvh] = carry

    def pages_cat(refs, kvh):
        return jnp.concatenate([r[kvh].astype(BF16) for r in refs], axis=1)

    def pv_pages(a, vt_cat):
        a_cat = jnp.concatenate([a[i * rows:(i + 1) * rows] for i in range(npg)], axis=1)
        return _dot_nt(a_cat, vt_cat)

    zs = []
    for kvh in heads:
        zc = _dot(q4_ref[kvh], pages_cat(k_refs, kvh))
        bt = bt_ref[kvh]
        zs.append(jnp.concatenate(
            [zc[:, i * PAGE_SIZE:(i + 1) * PAGE_SIZE] + bt for i in range(npg)], axis=0))
    for kvh in heads:
        pv, carry = attend(zs[kvh], pages_cat(v_refs, kvh), pv_pages, car_ref[kvh], None, npg)
        acc_ref[kvh] += pv
        car_ref[kvh] = carry

    @pl.when(c == pl.num_programs(1) - 1)
    def _():
        for kvh in heads:
            for g in range(GROUP):
                o_ref[:, _head_cols(kvh * GROUP + g)] = acc_ref[kvh, g * t:(g + 1) * t, :].astype(BF16)


def sb_sample(q_s, k_new, v_new, pool_k, pool_v, page_table, layer, bias):
    db, t, _ = q_s.shape
    n_pages = page_table.shape[1]
    npg = PAGES_PER_STEP
    nc = n_pages // npg
    rows = GROUP * t
    seq_blk = lambda b, c, pt: (b, 0, 0)
    seq_blk4 = lambda b, c, pt: (b, 0, 0, 0)

    def page_spec(i):
        return pl.BlockSpec(
            (None, None, N_KV_HEADS, HEAD_DIM, PAGE_SIZE),
            lambda b, c, pt: (layer, pt[b * n_pages + (n_pages - 1 - (c * npg + i))], 0, 0, 0))

    grid_spec = pltpu.PrefetchScalarGridSpec(
        num_scalar_prefetch=1,
        grid=(db, nc),
        in_specs=[pl.BlockSpec(memory_space=pltpu.SMEM),
                  pl.BlockSpec((None, t, Q_DIM), seq_blk),
                  pl.BlockSpec((None, t, N_KV_HEADS, HEAD_DIM), seq_blk4),
                  pl.BlockSpec((None, t, N_KV_HEADS, HEAD_DIM), seq_blk4),
                  pl.BlockSpec((PAGE_SIZE, 2 * PAGE_SIZE), lambda b, c, pt: (0, 0))]
                 + [page_spec(i) for i in range(npg)] * 2,
        out_specs=pl.BlockSpec((None, t, Q_DIM), seq_blk),
        scratch_shapes=[pltpu.VMEM((N_KV_HEADS, rows, HEAD_DIM), BF16),
                        pltpu.VMEM((N_KV_HEADS, rows, PAGE_SIZE), F32),
                        pltpu.VMEM((N_KV_HEADS, rows, HEAD_DIM), F32),
                        pltpu.VMEM((N_KV_HEADS, rows, PAGE_SIZE), F32)])
    return pl.pallas_call(
        functools.partial(_sb_sample_kernel, t=t, npg=npg),
        grid_spec=grid_spec,
        out_shape=jax.ShapeDtypeStruct((db, t, Q_DIM), BF16),
        compiler_params=_cparams(("parallel", "arbitrary")),
        name="sb_sample",
    )(page_table.reshape(-1), bias, q_s, k_new, v_new, _later_keys_matrix(PAGE_SIZE, True),
      *([pool_k] * npg), *([pool_v] * npg))


def _alibi_slopes():
    h = np.arange(1, N_HEADS + 1, dtype=np.float64)
    return np.exp2(-8.0 * h / N_HEADS).reshape(N_KV_HEADS, GROUP)


def _swa_tile(t_q, dist):
    valid = (dist >= 0) & (dist < WINDOW)
    tile = np.where(valid[None, None], -_alibi_slopes()[:, :, None, None] * dist[None, None], -np.inf)
    return jnp.asarray(tile.reshape(N_KV_HEADS, GROUP * t_q, dist.shape[1]), dtype=F32)


def _swa_probs(s, sink):
    m = jnp.maximum(jnp.max(s, axis=1, keepdims=True), sink)
    p = jnp.exp(s - m)
    denom = jnp.sum(p, axis=1, keepdims=True) + jnp.exp(sink - m)
    return (p * (1.0 / denom)).astype(BF16)


def _sink_col(sinks_ref, kvh, rows, t):
    row_g = lax.broadcasted_iota(jnp.int32, (rows, 1), 0) // t
    col = jnp.zeros((rows, 1), F32)
    for g in range(GROUP):
        col = jnp.where(row_g == g, sinks_ref[kvh * GROUP + g], col)
    return col


def _swa_prompt_kernel(sinks_ref, q_ref, kp_ref, kc_ref, vp_ref, vc_ref, tile_ref, o_ref, *, tq):
    qi = pl.program_id(1)
    rows = GROUP * tq
    col = lax.broadcasted_iota(jnp.int32, (rows, 2 * tq), 1)
    keep = (col >= tq) | (qi > 0)
    vband = jnp.concatenate([vp_ref[...], vc_ref[...]], axis=0)
    for kvh in range(N_KV_HEADS):
        q4 = jnp.concatenate([q_ref[:, _head_cols(kvh * GROUP + g)] for g in range(GROUP)], axis=0)
        kband_t = jnp.concatenate([kp_ref[kvh, :HEAD_DIM], kc_ref[kvh, :HEAD_DIM]], axis=1)
        s = _dot(q4, kband_t) + jnp.where(keep, tile_ref[kvh], NEG_INF)
        o = _dot(_swa_probs(s, _sink_col(sinks_ref, kvh, rows, tq)),
                 vband[:, kvh * LANES:kvh * LANES + HEAD_DIM])
        for g in range(GROUP):
            o_ref[:, _head_cols(kvh * GROUP + g)] = o[g * tq:(g + 1) * tq].astype(BF16)


def swa_prompt(q, kb, vb, sinks, batch, seq):
    tq = WINDOW
    nq = seq // tq
    qi = np.arange(tq)[:, None]
    kj = np.arange(2 * tq)[None, :]
    tile = _swa_tile(tq, tq + qi - kj)
    cur = lambda b, i: (b * nq + i, 0)
    prev = lambda b, i: (b * nq + jnp.maximum(i - 1, 0), 0)
    cur_t = lambda b, i: (b, 0, 0, i)
    prev_t = lambda b, i: (b, 0, 0, jnp.maximum(i - 1, 0))
    return pl.pallas_call(
        functools.partial(_swa_prompt_kernel, tq=tq),
        grid=(batch, nq),
        in_specs=[pl.BlockSpec(memory_space=pltpu.SMEM),
                  pl.BlockSpec((tq, Q_DIM), cur),
                  pl.BlockSpec((None, N_KV_HEADS, LANES, tq), prev_t),
                  pl.BlockSpec((None, N_KV_HEADS, LANES, tq), cur_t),
                  pl.BlockSpec((tq, N_KV_HEADS * LANES), prev),
                  pl.BlockSpec((tq, N_KV_HEADS * LANES), cur),
                  pl.BlockSpec((N_KV_HEADS, GROUP * tq, 2 * tq), lambda b, i: (0, 0, 0))],
        out_specs=pl.BlockSpec((tq, Q_DIM), cur),
        out_shape=jax.ShapeDtypeStruct((batch * seq, Q_DIM), BF16),
        compiler_params=_cparams(("parallel", "parallel")),
        name="swa_prompt",
    )(sinks, q, kb, kb, vb, vb, tile)


def _swa_sample_kernel(sinks_ref, q_ref, kn_ref, vn_ref, kbuf_ref, vbuf_ref, tile_ref,
                       o_ref, ko_ref, vo_ref, *, t, w_buf):
    rows = GROUP * t
    row_pad = jnp.zeros((w_buf - t, HEAD_DIM), F32)
    lane_pad = jnp.zeros((w_buf, LANES - HEAD_DIM), F32)
    sink_cols = [_sink_col(sinks_ref, kvh, rows, t) for kvh in range(N_KV_HEADS)]

    def shifted(buf_t, new):
        new_t = jnp.concatenate([new, lane_pad], axis=1).T[:HEAD_DIM, :t]
        return jnp.concatenate([buf_t[:, t:], new_t], axis=1)

    for s in range(q_ref.shape[0]):
        q = q_ref[s].astype(F32)
        for kvh in range(N_KV_HEADS):
            kbuf_t = kbuf_ref[s, kvh]
            vbuf_t = vbuf_ref[s, kvh]
            kn = jnp.concatenate([kn_ref[s, :, kvh, :], row_pad], axis=0)
            vn = jnp.concatenate([vn_ref[s, :, kvh, :], row_pad], axis=0)
            ko_ref[s, kvh] = shifted(kbuf_t, kn)
            vo_ref[s, kvh] = shifted(vbuf_t, vn)
            q4 = jnp.concatenate(
                [q[:, _head_cols(kvh * GROUP + g)] for g in range(GROUP)], axis=0).astype(BF16)
            sc = jnp.concatenate([_dot(q4, kbuf_t.astype(BF16)), _dot_nt(q4, kn.astype(BF16))], axis=1)
            p = _swa_probs(sc + tile_ref[kvh], sink_cols[kvh])
            o = _dot_nt(p[:, :w_buf], vbuf_t.astype(BF16)) + _dot(p[:, w_buf:], vn.astype(BF16))
            for g in range(GROUP):
                o_ref[s, :, _head_cols(kvh * GROUP + g)] = o[g * t:(g + 1) * t].astype(BF16)


def swa_sample(q_s, k_new, v_new, buf_k, buf_v, layer, sinks):
    db, t, _ = q_s.shape
    w_buf = buf_k.shape[-1]
    ns = SWA_SEQS_PER_STEP
    idx = np.arange(2 * w_buf)[None, :]
    dist = np.where(idx < w_buf + t, (w_buf + np.arange(t))[:, None] - idx, -1)
    tile = _swa_tile(t, dist)
    seq_blk = lambda i: (i, 0, 0)
    seq_blk4 = lambda i: (i, 0, 0, 0)
    buf_blk = lambda i: (layer, i, 0, 0, 0)
    kv_shape = (ns, N_KV_HEADS, HEAD_DIM, w_buf)
    return pl.pallas_call(
        functools.partial(_swa_sample_kernel, t=t, w_buf=w_buf),
        grid=(db // ns,),
        in_specs=[pl.BlockSpec(memory_space=pltpu.SMEM),
                  pl.BlockSpec((ns, t, Q_DIM), seq_blk),
                  pl.BlockSpec((ns, t, N_KV_HEADS, HEAD_DIM), seq_blk4),
                  pl.BlockSpec((ns, t, N_KV_HEADS, HEAD_DIM), seq_blk4),
                  pl.BlockSpec((None,) + kv_shape, buf_blk),
                  pl.BlockSpec((None,) + kv_shape, buf_blk),
                  pl.BlockSpec((N_KV_HEADS, GROUP * t, 2 * w_buf), lambda i: (0, 0, 0))],
        out_specs=[pl.BlockSpec((ns, t, Q_DIM), seq_blk),
                   pl.BlockSpec(kv_shape, seq_blk4),
                   pl.BlockSpec(kv_shape, seq_blk4)],
        out_shape=[jax.ShapeDtypeStruct((db, t, Q_DIM), BF16),
                   jax.ShapeDtypeStruct((db, N_KV_HEADS, HEAD_DIM, w_buf), F32),
                   jax.ShapeDtypeStruct((db, N_KV_HEADS, HEAD_DIM, w_buf), F32)],
        compiler_params=_cparams(("parallel",)),
        name="swa_sample",
    )(sinks, q_s, k_new, v_new, buf_k, buf_v, tile)


def _pad_router(w):
    w = jnp.pad(w, ((0, 0), (0, ROUTER_LANES - w.shape[1])))
    hi = w.astype(BF16)
    return hi, (w - hi.astype(F32)).astype(BF16)


def _token_tile(n_p, n_s):
    for tm in (512, 256, 128, 64, 32, 16, 8):
        if n_p % tm == 0 and n_s % tm == 0:
            return tm
    raise ValueError(f"token counts {n_p}, {n_s} are not multiples of 8")


def kernel(x_prompt, x_sample, cache_sb_k, cache_sb_v, cache_swa_k, cache_swa_v, page_table,
           p_prompt, p_sample, attn_norm, w_qkv, w_o, sb_bias, attn_sinks, ffn_norm, w_gu_dense,
           w_down_dense, w_router, w_gu_expert, w_down_expert, ple_norm, w_ple_proj,
           w_ple_gate, final_norm):
    batch, seq, _ = x_prompt.shape
    db, t, _ = x_sample.shape
    depth = attn_norm.shape[0]
    n_p = batch * seq
    n_s = db * t
    tm = _token_tile(n_p, n_s)
    assert seq % SB_KEY_BLOCK == 0 and seq % WINDOW == 0

    h = jnp.concatenate([x_prompt.reshape(n_p, D_MODEL), x_sample.reshape(n_s, D_MODEL)], axis=0)
    p_all = jnp.concatenate([p_prompt.reshape(depth, n_p, -1), p_sample.reshape(depth, n_s, -1)], axis=1)
    w_keep = min(WINDOW, seq)
    d_ff = w_down_dense.shape[1]
    d_ffe = w_down_expert.shape[2]
    to_t = lambda x: jnp.moveaxis(x, -3, -1)
    from_t = lambda x: jnp.moveaxis(x, -1, -3)
    pool_k, pool_v = to_t(cache_sb_k), to_t(cache_sb_v)
    buf_k, buf_v = to_t(cache_swa_k), to_t(cache_swa_v)

    sb_kp, sb_vp, sb_ks, sb_vs = [], [], [], []
    sw_kp, sw_vp, sw_ks, sw_vs = [], [], [], []
    for i in range(depth):
        j = i // 2
        g_attn = attn_norm[i][None]
        w_i = w_qkv[i].astype(BF16)
        q_p, kt_p, vt_p, kb, vb = qkv_prompt(h, g_attn, w_i, batch, seq, tm)
        q_s, k_s, v_s = qkv_sample(h, g_attn, w_i, n_p, n_s, tm)
        q_s = q_s.reshape(db, t, Q_DIM)
        k_s = k_s.reshape(db, t, N_KV_HEADS, HEAD_DIM)
        v_s = v_s.reshape(db, t, N_KV_HEADS, HEAD_DIM)
        if i % 2 == 0:
            o_p = sb_prompt(q_p, kb, vb, sb_bias[j], batch, seq, PAGE_SIZE, SB_KEY_BLOCK)
            o_s = sb_sample(q_s, k_s, v_s, pool_k, pool_v, page_table, j, sb_bias[j])
            sb_kp.append(from_t(kt_p)); sb_vp.append(from_t(vt_p)); sb_ks.append(k_s); sb_vs.append(v_s)
        else:
            o_p = swa_prompt(q_p, kb, vb, attn_sinks[j], batch, seq)
            o_s, bks, bvs = swa_sample(q_s, k_s, v_s, buf_k, buf_v, j, attn_sinks[j])
            sw_kp.append(from_t(kt_p[..., seq - w_keep:])); sw_vp.append(from_t(vt_p[..., seq - w_keep:]))
            sw_ks.append(from_t(bks)); sw_vs.append(from_t(bvs))
        o_s = o_s.reshape(n_s, Q_DIM)
        if i % 2 == 0:
            h1, xn = post_attn(h, o_p, o_s, w_o[i].astype(BF16), ffn_norm[i][None], None, tm)
            h2 = dense_ffn(h1, xn, w_gu_dense[j].astype(BF16), w_down_dense[j].astype(BF16), tm, d_ff // 2)
        else:
            h1, xn, route = post_attn(h, o_p, o_s, w_o[i].astype(BF16), ffn_norm[i][None],
                                      _pad_router(w_router[j]), tm)
            h2 = moe_ffn(h1, xn, route, w_gu_expert[j].astype(BF16), w_down_expert[j].astype(BF16),
                         tm, d_ffe // 2)
        h = ple_add(h2, p_all[i], ple_norm[i][None], w_ple_proj[i].astype(BF16),
                    w_ple_gate[i].astype(BF16), final_norm[None] if i == depth - 1 else None, tm)

    y_prompt = h[:n_p].reshape(batch, seq, D_MODEL)
    y_sample = h[n_p:].reshape(db, t, D_MODEL)
    return (y_prompt, y_sample,
            jnp.stack(sb_kp), jnp.stack(sb_vp), jnp.stack(sb_ks), jnp.stack(sb_vs),
            jnp.stack(sw_kp), jnp.stack(sw_vp), jnp.stack(sw_ks), jnp.stack(sw_vs))
```

```python
import functools

import numpy as np
import jax
import jax.numpy as jnp
from jax import lax
from jax.experimental import pallas as pl
from jax.experimental.pallas import tpu as pltpu

F32 = jnp.float32
BF16 = jnp.bfloat16

D_MODEL = 1024
N_HEADS = 16
HEAD_DIM = 64
N_KV_HEADS = 4
GROUP = N_HEADS // N_KV_HEADS
Q_DIM = N_HEADS * HEAD_DIM
KV_DIM = N_KV_HEADS * HEAD_DIM
QKV_DIM = Q_DIM + 2 * KV_DIM
ATTN_SCALE = HEAD_DIM ** -0.5
WINDOW = 128
PAGE_SIZE = 128
N_EXPERTS = 8
TOP_K = 2
RMS_EPS = 1e-6

LANES = 128
ROUTER_LANES = LANES
VMEM_LIMIT = 56 * 1024 * 1024
PAGES_PER_STEP = 32
SWA_SEQS_PER_STEP = 8
SB_KEY_BLOCK = 256
SB_QUERY_BLOCK = 256
SB_ROW_CHUNK = GROUP * SB_QUERY_BLOCK
DMA_LOOP_UNROLL = 8
NEG_INF = float("-inf")
BIAS_PARTS = 3
LANE_EXPERT = N_EXPERTS
LANE_GATE = N_EXPERTS + TOP_K


def _cparams(sem, vmem=VMEM_LIMIT):
    return pltpu.CompilerParams(dimension_semantics=sem, vmem_limit_bytes=vmem)


def _dot(a, b):
    return jnp.dot(a, b, preferred_element_type=F32)


def _dot_nt(a, b):
    return lax.dot_general(a, b, (((1,), (1,)), ((), ())), preferred_element_type=F32)


def _head_cols(h):
    return slice(h * HEAD_DIM, (h + 1) * HEAD_DIM)


def _rms(x, g):
    ms = jnp.mean(x * x, axis=-1, keepdims=True)
    return x * lax.rsqrt(ms + RMS_EPS) * g


def _neg_abs(z):
    return lax.bitcast_convert_type(lax.bitcast_convert_type(z, jnp.int32) | jnp.int32(-2 ** 31), F32)


def _softplus(z):
    return jnp.maximum(z, 0.0) + jnp.log(1.0 + jnp.exp(_neg_abs(z)))


def _qkv(h_ref, g_ref, w_ref, q_ref):
    xn = _rms(h_ref[...], g_ref[...]).astype(BF16)
    t = _dot(xn, w_ref[...])
    q_ref[...] = (t[:, :Q_DIM] * ATTN_SCALE).astype(BF16)
    return t[:, Q_DIM:Q_DIM + KV_DIM], t[:, Q_DIM + KV_DIM:]


def _qkv_prompt_kernel(h_ref, g_ref, w_ref, q_ref, kt_ref, vt_ref, kb_ref, vb_ref):
    k, v = _qkv(h_ref, g_ref, w_ref, q_ref)
    kt = k.T
    vt = v.T
    row = lax.broadcasted_iota(jnp.int32, (LANES - HEAD_DIM, k.shape[0]), 0)
    ones = jnp.where(row < BIAS_PARTS, 1.0, 0.0).astype(BF16)
    for kvh in range(N_KV_HEADS):
        kt_ref[kvh] = kt[_head_cols(kvh)]
        vt_ref[kvh] = vt[_head_cols(kvh)]
        kb_ref[kvh, :HEAD_DIM] = kt[_head_cols(kvh)].astype(BF16)
        kb_ref[kvh, HEAD_DIM:] = ones
        vb_ref[:, kvh * LANES:kvh * LANES + HEAD_DIM] = v[:, _head_cols(kvh)].astype(BF16)
        vb_ref[:, kvh * LANES + HEAD_DIM:(kvh + 1) * LANES] = jnp.zeros((v.shape[0], LANES - HEAD_DIM), BF16)


def _qkv_sample_kernel(h_ref, g_ref, w_ref, q_ref, k_ref, v_ref):
    k, v = _qkv(h_ref, g_ref, w_ref, q_ref)
    for kvh in range(N_KV_HEADS):
        k_ref[:, kvh, :] = k[:, _head_cols(kvh)]
        v_ref[:, kvh, :] = v[:, _head_cols(kvh)]


def _qkv_in_specs(tm, row):
    return [pl.BlockSpec((tm, D_MODEL), row),
            pl.BlockSpec((1, D_MODEL), lambda *_: (0, 0)),
            pl.BlockSpec((D_MODEL, QKV_DIM), lambda *_: (0, 0))]


def qkv_prompt(h, g, w, batch, seq, tm):
    ns = seq // tm
    row = lambda b, i: (b * ns + i, 0)
    tblk = lambda b, i: (b, 0, 0, i)
    return pl.pallas_call(
        _qkv_prompt_kernel,
        grid=(batch, ns),
        in_specs=_qkv_in_specs(tm, row),
        out_specs=[pl.BlockSpec((tm, Q_DIM), row),
                   pl.BlockSpec((None, N_KV_HEADS, HEAD_DIM, tm), tblk),
                   pl.BlockSpec((None, N_KV_HEADS, HEAD_DIM, tm), tblk),
                   pl.BlockSpec((None, N_KV_HEADS, LANES, tm), tblk),
                   pl.BlockSpec((tm, N_KV_HEADS * LANES), row)],
        out_shape=[jax.ShapeDtypeStruct((batch * seq, Q_DIM), BF16),
                   jax.ShapeDtypeStruct((batch, N_KV_HEADS, HEAD_DIM, seq), F32),
                   jax.ShapeDtypeStruct((batch, N_KV_HEADS, HEAD_DIM, seq), F32),
                   jax.ShapeDtypeStruct((batch, N_KV_HEADS, LANES, seq), BF16),
                   jax.ShapeDtypeStruct((batch * seq, N_KV_HEADS * LANES), BF16)],
        compiler_params=_cparams(("parallel", "parallel")),
        name="qkv_prompt",
    )(h, g, w)


def qkv_sample(h, g, w, first_row, n_rows, tm):
    first = first_row // tm
    row_in = lambda i: (first + i, 0)
    row = lambda i: (i, 0)
    row3 = lambda i: (i, 0, 0)
    return pl.pallas_call(
        _qkv_sample_kernel,
        grid=(n_rows // tm,),
        in_specs=_qkv_in_specs(tm, row_in),
        out_specs=[pl.BlockSpec((tm, Q_DIM), row),
                   pl.BlockSpec((tm, N_KV_HEADS, HEAD_DIM), row3),
                   pl.BlockSpec((tm, N_KV_HEADS, HEAD_DIM), row3)],
        out_shape=[jax.ShapeDtypeStruct((n_rows, Q_DIM), BF16),
                   jax.ShapeDtypeStruct((n_rows, N_KV_HEADS, HEAD_DIM), F32),
                   jax.ShapeDtypeStruct((n_rows, N_KV_HEADS, HEAD_DIM), F32)],
        compiler_params=_cparams(("parallel",)),
        name="qkv_sample",
    )(h, g, w)


def _route(xn, wh_ref, wl_ref):
    xh = xn.astype(BF16)
    xl = (xn - xh.astype(F32)).astype(BF16)
    wh = wh_ref[...]
    logits = _dot(xh, wh) + _dot(xl, wh) + _dot(xh, wl_ref[...])
    lane = lax.broadcasted_iota(jnp.int32, logits.shape, 1)
    logits = jnp.where(lane < N_EXPERTS, logits, NEG_INF)
    m1 = jnp.max(logits, axis=1, keepdims=True)
    i1 = jnp.min(jnp.where(logits == m1, lane, ROUTER_LANES), axis=1, keepdims=True)
    rest = jnp.where(lane == i1, NEG_INF, logits)
    m2 = jnp.max(rest, axis=1, keepdims=True)
    i2 = jnp.min(jnp.where(rest == m2, lane, ROUTER_LANES), axis=1, keepdims=True)
    e2 = jnp.exp(m2 - m1)
    g1 = 1.0 / (1.0 + e2)
    out = jnp.where(lane == LANE_EXPERT, i1.astype(F32), 0.0)
    out = jnp.where(lane == LANE_EXPERT + 1, i2.astype(F32), out)
    out = jnp.where(lane == LANE_GATE, g1, out)
    return jnp.where(lane == LANE_GATE + 1, e2 * g1, out)


def _post_attn_kernel(h_ref, ap_ref, as_ref, wo_ref, g_ref, *rest, routed, prompt_tiles):
    if routed:
        wh_ref, wl_ref, h1_ref, xn_ref, route_ref = rest
    else:
        h1_ref, xn_ref = rest
    attn = jnp.where(pl.program_id(0) < prompt_tiles, ap_ref[...], as_ref[...])
    h1 = h_ref[...] + _dot(attn, wo_ref[...])
    h1_ref[...] = h1
    xn = _rms(h1, g_ref[...])
    xn_ref[...] = xn.astype(xn_ref.dtype)
    if routed:
        route_ref[...] = _route(xn, wh_ref, wl_ref)


def post_attn(h, attn_p, attn_s, wo, g, router, tm):
    n = h.shape[0]
    prompt_tiles = attn_p.shape[0] // tm
    sample_tiles = attn_s.shape[0] // tm
    row = lambda i: (i, 0)
    fixed = lambda i: (0, 0)
    routed = router is not None
    in_specs = [pl.BlockSpec((tm, D_MODEL), row),
                pl.BlockSpec((tm, Q_DIM), lambda i: (jnp.minimum(i, prompt_tiles - 1), 0)),
                pl.BlockSpec((tm, Q_DIM), lambda i: (jnp.clip(i - prompt_tiles, 0, sample_tiles - 1), 0)),
                pl.BlockSpec((Q_DIM, D_MODEL), fixed),
                pl.BlockSpec((1, D_MODEL), fixed)]
    out_specs = [pl.BlockSpec((tm, D_MODEL), row), pl.BlockSpec((tm, D_MODEL), row)]
    out_shape = [jax.ShapeDtypeStruct((n, D_MODEL), F32),
                 jax.ShapeDtypeStruct((n, D_MODEL), F32 if routed else BF16)]
    args = [h, attn_p, attn_s, wo, g]
    if routed:
        in_specs += [pl.BlockSpec((D_MODEL, ROUTER_LANES), fixed)] * 2
        out_specs.append(pl.BlockSpec((tm, ROUTER_LANES), row))
        out_shape.append(jax.ShapeDtypeStruct((n, ROUTER_LANES), F32))
        args += list(router)
    return pl.pallas_call(
        functools.partial(_post_attn_kernel, routed=routed, prompt_tiles=prompt_tiles),
        grid=(n // tm,),
        in_specs=in_specs, out_specs=out_specs, out_shape=out_shape,
        compiler_params=_cparams(("parallel",)),
        name="post_attn_routed" if routed else "post_attn",
    )(*args)


def _swiglu_partial(x, wg, wu, wd):
    g = _dot(x, wg)
    u = _dot(x, wu)
    return _dot((g * jax.nn.sigmoid(g) * u).astype(BF16), wd)


def _ffn_kernel(h_ref, x_ref, wg_ref, wu_ref, wd_ref, o_ref, acc_ref):
    f = pl.program_id(1)

    @pl.when(f == 0)
    def _():
        acc_ref[...] = jnp.zeros_like(acc_ref)

    acc_ref[...] += _swiglu_partial(x_ref[...], wg_ref[...], wu_ref[...], wd_ref[...])

    @pl.when(f == pl.num_programs(1) - 1)
    def _():
        o_ref[...] = h_ref[...] + acc_ref[...]


def dense_ffn(h, xn, w_gu, w_down, tm, tf):
    n = h.shape[0]
    d_ff = w_down.shape[0]
    nf = d_ff // tf
    row = lambda i, f: (i, 0)
    return pl.pallas_call(
        _ffn_kernel,
        grid=(n // tm, nf),
        in_specs=[pl.BlockSpec((tm, D_MODEL), row),
                  pl.BlockSpec((tm, D_MODEL), row),
                  pl.BlockSpec((D_MODEL, tf), lambda i, f: (0, f)),
                  pl.BlockSpec((D_MODEL, tf), lambda i, f: (0, nf + f)),
                  pl.BlockSpec((tf, D_MODEL), lambda i, f: (f, 0))],
        out_specs=pl.BlockSpec((tm, D_MODEL), row),
        out_shape=jax.ShapeDtypeStruct((n, D_MODEL), F32),
        scratch_shapes=[pltpu.VMEM((tm, D_MODEL), F32)],
        compiler_params=_cparams(("parallel", "arbitrary")),
        name="dense_ffn",
    )(h, xn, w_gu, w_gu, w_down)


def _ple_kernel(h_ref, p_ref, g_ref, wp_ref, wg_ref, *rest, final):
    if final:
        gf_ref, o_ref = rest
    else:
        (o_ref,) = rest
    h = h_ref[...]
    gate = jax.nn.sigmoid(_dot(_rms(h, g_ref[...]).astype(BF16), wg_ref[...]))
    out = h + _dot(p_ref[...].astype(BF16), wp_ref[...]) * gate
    if final:
        out = _rms(out, gf_ref[...])
    o_ref[...] = out


def ple_add(h, p, g, w_proj, w_gate, final_g, tm):
    n = h.shape[0]
    ple_dim = p.shape[1]
    row = lambda i: (i, 0)
    fixed = lambda i: (0, 0)
    final = final_g is not None
    in_specs = [pl.BlockSpec((tm, D_MODEL), row),
                pl.BlockSpec((tm, ple_dim), row),
                pl.BlockSpec((1, D_MODEL), fixed),
                pl.BlockSpec((ple_dim, D_MODEL), fixed),
                pl.BlockSpec((D_MODEL, D_MODEL), fixed)]
    args = [h, p, g, w_proj, w_gate]
    if final:
        in_specs.append(pl.BlockSpec((1, D_MODEL), fixed))
        args.append(final_g)
    return pl.pallas_call(
        functools.partial(_ple_kernel, final=final),
        grid=(n // tm,),
        in_specs=in_specs,
        out_specs=pl.BlockSpec((tm, D_MODEL), row),
        out_shape=jax.ShapeDtypeStruct((n, D_MODEL), F32),
        compiler_params=_cparams(("parallel",)),
        name="ple_final" if final else "ple",
    )(*args)


def _row_copy(src_ref, src_row, dst_ref, dst_row, sem):
    return pltpu.make_async_copy(src_ref.at[pl.ds(src_row, 1)], dst_ref.at[pl.ds(dst_row, 1)], sem)


def _moe_dispatch_kernel(pos_ref, x_ref, xs_in_ref, xs_ref, sem, *, tm):
    del xs_in_ref

    def copies(r):
        return [_row_copy(x_ref, r, xs_ref, pos_ref[s, r], sem) for s in range(TOP_K)]

    def start(r, c):
        for cp in copies(r):
            cp.start()
        return c

    def wait(r, c):
        for cp in copies(r):
            cp.wait()
        return c

    lax.fori_loop(0, tm, start, 0, unroll=DMA_LOOP_UNROLL)
    lax.fori_loop(0, tm, wait, 0, unroll=DMA_LOOP_UNROLL)


def moe_dispatch(xn, pos_tiles, n_rows, tm):
    n = xn.shape[0]
    xs0 = jnp.zeros((n_rows, D_MODEL), F32)
    return pl.pallas_call(
        functools.partial(_moe_dispatch_kernel, tm=tm),
        grid=(n // tm,),
        in_specs=[pl.BlockSpec((None, TOP_K, tm), lambda i: (i, 0, 0), memory_space=pltpu.SMEM),
                  pl.BlockSpec((tm, D_MODEL), lambda i: (i, 0)),
                  pl.BlockSpec(memory_space=pl.ANY)],
        out_specs=pl.BlockSpec(memory_space=pl.ANY),
        out_shape=jax.ShapeDtypeStruct((n_rows, D_MODEL), F32),
        scratch_shapes=[pltpu.SemaphoreType.DMA(())],
        input_output_aliases={2: 0},
        compiler_params=_cparams(("arbitrary",)),
        name="moe_dispatch",
    )(pos_tiles, xn, xs0)


def _moe_grouped_kernel(te_ref, nt_ref, x_ref, wg_ref, wu_ref, wd_ref, o_ref, acc_ref):
    i = pl.program_id(0)
    f = pl.program_id(1)
    live = i < nt_ref[0]

    @pl.when(live & (f == 0))
    def _():
        acc_ref[...] = jnp.zeros_like(acc_ref)

    @pl.when(live)
    def _():
        acc_ref[...] += _swiglu_partial(x_ref[...].astype(BF16), wg_ref[0], wu_ref[0], wd_ref[0])

    @pl.when(live & (f == pl.num_programs(1) - 1))
    def _():
        o_ref[...] = acc_ref[...]

    @pl.when(jnp.logical_not(live) & (f == pl.num_programs(1) - 1))
    def _():
        o_ref[...] = jnp.zeros_like(o_ref)


def moe_grouped(xs, tile_expert, n_tiles, w_gu, w_down, tm, tf):
    n_rows = xs.shape[0]
    max_tiles = n_rows // tm
    d_ff = w_down.shape[1]
    nf = d_ff // tf
    last = lambda i, nt: jnp.minimum(i, nt[0] - 1)
    fblk = lambda i, f, nt: jnp.where(i < nt[0], f, nf - 1)
    row = lambda i, f, te, nt: (last(i, nt), 0)
    grid_spec = pltpu.PrefetchScalarGridSpec(
        num_scalar_prefetch=2,
        grid=(max_tiles, nf),
        in_specs=[pl.BlockSpec((tm, D_MODEL), row),
                  pl.BlockSpec((1, D_MODEL, tf), lambda i, f, te, nt: (te[last(i, nt)], 0, fblk(i, f, nt))),
                  pl.BlockSpec((1, D_MODEL, tf), lambda i, f, te, nt: (te[last(i, nt)], 0, nf + fblk(i, f, nt))),
                  pl.BlockSpec((1, tf, D_MODEL), lambda i, f, te, nt: (te[last(i, nt)], fblk(i, f, nt), 0))],
        out_specs=pl.BlockSpec((tm, D_MODEL), lambda i, f, te, nt: (i, 0)),
        scratch_shapes=[pltpu.VMEM((tm, D_MODEL), F32)])
    return pl.pallas_call(
        _moe_grouped_kernel,
        grid_spec=grid_spec,
        out_shape=jax.ShapeDtypeStruct((n_rows, D_MODEL), F32),
        compiler_params=_cparams(("arbitrary", "arbitrary")),
        name="moe_grouped",
    )(tile_expert, n_tiles, xs, w_gu, w_gu, w_down)


def _moe_combine_kernel(pos_ref, h_ref, route_ref, ys_ref, o_ref, ybuf_ref, sem, *, tm):
    def copies(r):
        return [_row_copy(ys_ref, pos_ref[s, r], ybuf_ref.at[s], r, sem) for s in range(TOP_K)]

    def start(r, c):
        for cp in copies(r):
            cp.start()
        return c

    def wait(r, c):
        for cp in copies(r):
            cp.wait()
        return c

    lax.fori_loop(0, tm, start, 0, unroll=DMA_LOOP_UNROLL)
    route = route_ref[...]
    lane = lax.broadcasted_iota(jnp.int32, route.shape, 1)
    gates = [jnp.sum(jnp.where(lane == LANE_GATE + s, route, 0.0), axis=1, keepdims=True)
             for s in range(TOP_K)]
    lax.fori_loop(0, tm, wait, 0, unroll=DMA_LOOP_UNROLL)
    out = gates[0] * ybuf_ref[0]
    for s in range(1, TOP_K):
        out = out + gates[s] * ybuf_ref[s]
    o_ref[...] = h_ref[...] + out


def moe_combine(h, route, ys, pos_tiles, tm):
    n = h.shape[0]
    row = lambda i: (i, 0)
    return pl.pallas_call(
        functools.partial(_moe_combine_kernel, tm=tm),
        grid=(n // tm,),
        in_specs=[pl.BlockSpec((None, TOP_K, tm), lambda i: (i, 0, 0), memory_space=pltpu.SMEM),
                  pl.BlockSpec((tm, D_MODEL), row),
                  pl.BlockSpec((tm, ROUTER_LANES), row),
                  pl.BlockSpec(memory_space=pl.ANY)],
        out_specs=pl.BlockSpec((tm, D_MODEL), row),
        out_shape=jax.ShapeDtypeStruct((n, D_MODEL), F32),
        scratch_shapes=[pltpu.VMEM((TOP_K, tm, D_MODEL), F32), pltpu.SemaphoreType.DMA(())],
        compiler_params=_cparams(("arbitrary",)),
        name="moe_combine",
    )(pos_tiles, h, route, ys)


def _routing_tables(route, tm):
    n = route.shape[0]
    max_tiles = -(-(TOP_K * n + N_EXPERTS * (tm - 1)) // tm)
    expert = jnp.concatenate([route[:, LANE_EXPERT + s] for s in range(TOP_K)]).astype(jnp.int32)
    onehot = (expert[:, None] == jnp.arange(N_EXPERTS, dtype=jnp.int32)[None, :]).astype(jnp.int32)
    csum = jnp.cumsum(onehot, axis=0)
    counts = csum[-1]
    tiles = (counts + tm - 1) // tm
    tile_end = jnp.cumsum(tiles)
    row_start = (tile_end - tiles) * tm
    pos = jnp.sum(onehot * (csum - 1 + row_start[None, :]), axis=1)
    pos_tiles = pos.reshape(TOP_K, n // tm, tm).transpose(1, 0, 2)
    tile_expert = jnp.sum(jnp.arange(max_tiles, dtype=jnp.int32)[:, None] >= tile_end[None, :], axis=1)
    tile_expert = jnp.minimum(tile_expert, N_EXPERTS - 1).astype(jnp.int32)
    return pos_tiles, tile_expert, tile_end[-1:].astype(jnp.int32), max_tiles * tm


def moe_ffn(h, xn, route, w_gu, w_down, tm, tf):
    pos_tiles, tile_expert, n_tiles, n_rows = _routing_tables(route, tm)
    xs = moe_dispatch(xn, pos_tiles, n_rows, tm)
    ys = moe_grouped(xs, tile_expert, n_tiles, w_gu, w_down, tm, tf)
    return moe_combine(h, route, ys, pos_tiles, tm)


def _later_keys_matrix(tk, with_row_sum):
    j = np.arange(tk)[:, None]
    s = np.arange(2 * tk if with_row_sum else tk)[None, :]
    return jnp.asarray(np.where(s < tk, j > s, True), dtype=BF16)


def _split_bias(bias):
    parts = []
    rest = bias.astype(F32)
    for _ in range(BIAS_PARTS):
        p = rest.astype(BF16).astype(F32)
        parts.append(p)
        rest = rest - p
    return jnp.stack(parts)


def _sb_prompt_kernel(bias_ref, q_ref, k_ref, v_ref, m_ref, o_ref, q4_ref, acc_ref, car_ref, *, tq, tk):
    qi = pl.program_id(1)
    rows = GROUP * tq
    row_g = lax.broadcasted_iota(jnp.int32, (rows, HEAD_DIM), 0) // tq
    lane = lax.broadcasted_iota(jnp.int32, (rows, HEAD_DIM), 1)
    for kvh in range(N_KV_HEADS):
        q4 = jnp.concatenate([q_ref[:, _head_cols(kvh * GROUP + g)] for g in range(GROUP)], axis=0)
        bcols = jnp.zeros((rows, HEAD_DIM), F32)
        for g in range(GROUP):
            for part in range(BIAS_PARTS):
                bcols = jnp.where((row_g == g) & (lane == part), bias_ref[part, kvh * GROUP + g], bcols)
        q4_ref[kvh] = jnp.concatenate([q4, bcols.astype(BF16)], axis=1)
    acc_ref[...] = jnp.zeros_like(acc_ref)
    car_ref[...] = jnp.zeros_like(car_ref)

    def block(j, mask):
        start = pl.multiple_of(j * tk, tk)
        later = m_ref[...]
        for kvh in range(N_KV_HEADS):
            k_t = k_ref[kvh, :, pl.ds(start, tk)]
            v = v_ref[pl.ds(start, tk), kvh * LANES:(kvh + 1) * LANES]
            for r0 in range(0, rows, SB_ROW_CHUNK):
                rs = slice(r0, r0 + SB_ROW_CHUNK)
                z = _dot(q4_ref[kvh, rs], k_t)
                sp = _softplus(z)
                ls = z - sp
                if mask is not None:
                    sp = jnp.where(mask[rs], sp, 0.0)
                sp16 = sp.astype(BF16)
                tail = _dot(sp16, later)
                a = jnp.exp(ls - tail - car_ref[kvh, rs])
                if mask is not None:
                    a = jnp.where(mask[rs], a, 0.0)
                acc_ref[kvh, rs] += _dot(a.astype(BF16), v)
                car_ref[kvh, rs] += tail[:, :1] + sp16[:, :1].astype(F32)

    jd = (qi * tq) // tk
    t_pos = qi * tq + lax.broadcasted_iota(jnp.int32, (rows, tk), 0) % tq
    s_pos = jd * tk + lax.broadcasted_iota(jnp.int32, (rows, tk), 1)
    block(jd, s_pos < t_pos)

    def body(i, c):
        block(jd - 1 - i, None)
        return c

    lax.fori_loop(0, jd, body, 0)

    for kvh in range(N_KV_HEADS):
        for g in range(GROUP):
            o_ref[:, _head_cols(kvh * GROUP + g)] = acc_ref[kvh, g * tq:(g + 1) * tq, :HEAD_DIM].astype(BF16)


def sb_prompt(q, kb, vb, bias, batch, seq, tq, tk):
    nq = seq // tq
    rows = GROUP * tq
    return pl.pallas_call(
        functools.partial(_sb_prompt_kernel, tq=tq, tk=tk),
        grid=(batch, nq),
        in_specs=[pl.BlockSpec(memory_space=pltpu.SMEM),
                  pl.BlockSpec((tq, Q_DIM), lambda b, i: (b * nq + i, 0)),
                  pl.BlockSpec((None, N_KV_HEADS, LANES, seq), lambda b, i: (b, 0, 0, 0)),
                  pl.BlockSpec((seq, N_KV_HEADS * LANES), lambda b, i: (b, 0)),
                  pl.BlockSpec((tk, tk), lambda b, i: (0, 0))],
        out_specs=pl.BlockSpec((tq, Q_DIM), lambda b, i: (b * nq + i, 0)),
        out_shape=jax.ShapeDtypeStruct((batch * seq, Q_DIM), BF16),
        scratch_shapes=[pltpu.VMEM((N_KV_HEADS, rows, LANES), BF16),
                        pltpu.VMEM((N_KV_HEADS, rows, LANES), F32),
                        pltpu.VMEM((N_KV_HEADS, rows, 1), F32)],
        compiler_params=_cparams(("parallel", "parallel")),
        name="sb_prompt",
    )(_split_bias(bias), q, kb, vb, _later_keys_matrix(tk, False))


def _sb_sample_kernel(pt_ref, bias_ref, q_ref, kn_ref, vn_ref, m_ref, *rest, t, npg):
    k_refs = rest[:npg]
    v_refs = rest[npg:2 * npg]
    o_ref = rest[2 * npg]
    q4_ref, bt_ref, acc_ref, car_ref = rest[2 * npg + 1:]
    c = pl.program_id(1)
    rows = GROUP * t
    hrows = N_KV_HEADS * rows
    later = m_ref[...]
    heads = range(N_KV_HEADS)

    def attend(z, carry0, mask, n):
        sp = _softplus(z)
        if mask is not None:
            sp = jnp.where(mask, sp, 0.0)
        res = _dot(sp.astype(BF16), later)
        inc = res[:, PAGE_SIZE:]
        carry = carry0
        carries = []
        for i in range(n):
            carries.append(carry)
            carry = carry + inc[i * hrows:(i + 1) * hrows]
        a = jnp.exp(z - sp - res[:, :PAGE_SIZE] - jnp.concatenate(carries, axis=0))
        if mask is not None:
            a = jnp.where(mask, a, 0.0)
        return a.astype(BF16), carry

    @pl.when(c == 0)
    def _():
        q = q_ref[...].astype(F32)
        row_h = lax.broadcasted_iota(jnp.int32, (hrows, PAGE_SIZE), 0) // t
        t_in = lax.broadcasted_iota(jnp.int32, (hrows, PAGE_SIZE), 0) % t
        s_in = lax.broadcasted_iota(jnp.int32, (hrows, PAGE_SIZE), 1)
        bt = jnp.zeros((hrows, PAGE_SIZE), F32)
        for h in range(N_HEADS):
            bt = jnp.where(row_h == h, bias_ref[h], bt)
        bt_ref[...] = bt
        pad = jnp.zeros((PAGE_SIZE - t, HEAD_DIM), F32)
        z_new = []
        for kvh in heads:
            q4 = jnp.concatenate([q[:, _head_cols(kvh * GROUP + g)] for g in range(GROUP)], axis=0).astype(BF16)
            q4_ref[kvh] = q4
            kn = jnp.concatenate([kn_ref[:, kvh, :], pad], axis=0).astype(BF16)
            z_new.append(_dot_nt(q4, kn))
        a, carry = attend(jnp.concatenate(z_new, axis=0) + bt, jnp.zeros((hrows, PAGE_SIZE), F32),
                          s_in < t_in, 1)
        car_ref[...] = carry
        for kvh in heads:
            vn = jnp.concatenate([vn_ref[:, kvh, :], pad], axis=0).astype(BF16)
            acc_ref[kvh] = _dot(a[kvh * rows:(kvh + 1) * rows], vn)

    def pages_cat(refs, kvh):
        return jnp.concatenate([r[kvh].astype(BF16) for r in refs], axis=1)

    zc = [_dot(q4_ref[kvh], pages_cat(k_refs, kvh)) for kvh in heads]
    bt = bt_ref[...]
    z = jnp.concatenate(
        [jnp.concatenate([zc[kvh][:, i * PAGE_SIZE:(i + 1) * PAGE_SIZE] for kvh in heads], axis=0) + bt
         for i in range(npg)], axis=0)
    a, carry = attend(z, car_ref[...], None, npg)
    car_ref[...] = carry
    for kvh in heads:
        a_cat = jnp.concatenate(
            [a[i * hrows + kvh * rows:i * hrows + (kvh + 1) * rows] for i in range(npg)], axis=1)
        acc_ref[kvh] += _dot_nt(a_cat, pages_cat(v_refs, kvh))

    @pl.when(c == pl.num_programs(1) - 1)
    def _():
        for kvh in heads:
            for g in range(GROUP):
                o_ref[:, _head_cols(kvh * GROUP + g)] = acc_ref[kvh, g * t:(g + 1) * t, :].astype(BF16)


def sb_sample(q_s, k_new, v_new, pool_k, pool_v, page_table, layer, bias):
    db, t, _ = q_s.shape
    n_pages = page_table.shape[1]
    npg = PAGES_PER_STEP
    assert n_pages % npg == 0, (n_pages, npg)
    nc = n_pages // npg
    rows = GROUP * t
    seq_blk = lambda b, c, pt: (b, 0, 0)
    seq_blk4 = lambda b, c, pt: (b, 0, 0, 0)

    def page_spec(i):
        return pl.BlockSpec(
            (None, None, N_KV_HEADS, HEAD_DIM, PAGE_SIZE),
            lambda b, c, pt: (layer, pt[b * n_pages + (n_pages - 1 - (c * npg + i))], 0, 0, 0))

    grid_spec = pltpu.PrefetchScalarGridSpec(
        num_scalar_prefetch=1,
        grid=(db, nc),
        in_specs=[pl.BlockSpec(memory_space=pltpu.SMEM),
                  pl.BlockSpec((None, t, Q_DIM), seq_blk),
                  pl.BlockSpec((None, t, N_KV_HEADS, HEAD_DIM), seq_blk4),
                  pl.BlockSpec((None, t, N_KV_HEADS, HEAD_DIM), seq_blk4),
                  pl.BlockSpec((PAGE_SIZE, 2 * PAGE_SIZE), lambda b, c, pt: (0, 0))]
                 + [page_spec(i) for i in range(npg)] * 2,
        out_specs=pl.BlockSpec((None, t, Q_DIM), seq_blk),
        scratch_shapes=[pltpu.VMEM((N_KV_HEADS, rows, HEAD_DIM), BF16),
                        pltpu.VMEM((N_KV_HEADS * rows, PAGE_SIZE), F32),
                        pltpu.VMEM((N_KV_HEADS, rows, HEAD_DIM), F32),
                        pltpu.VMEM((N_KV_HEADS * rows, PAGE_SIZE), F32)])
    return pl.pallas_call(
        functools.partial(_sb_sample_kernel, t=t, npg=npg),
        grid_spec=grid_spec,
        out_shape=jax.ShapeDtypeStruct((db, t, Q_DIM), BF16),
        compiler_params=_cparams(("parallel", "arbitrary")),
        name="sb_sample",
    )(page_table.reshape(-1), bias, q_s, k_new, v_new, _later_keys_matrix(PAGE_SIZE, True),
      *([pool_k] * npg), *([pool_v] * npg))


def _alibi_slopes():
    h = np.arange(1, N_HEADS + 1, dtype=np.float64)
    return np.exp2(-8.0 * h / N_HEADS).reshape(N_KV_HEADS, GROUP)


def _swa_tile(t_q, dist):
    valid = (dist >= 0) & (dist < WINDOW)
    tile = np.where(valid[None, None], -_alibi_slopes()[:, :, None, None] * dist[None, None], -np.inf)
    return jnp.asarray(tile.reshape(N_KV_HEADS, GROUP * t_q, dist.shape[1]), dtype=F32)


def _swa_probs(s, sink):
    m = jnp.maximum(jnp.max(s, axis=1, keepdims=True), sink)
    p = jnp.exp(s - m)
    denom = jnp.sum(p, axis=1, keepdims=True) + jnp.exp(sink - m)
    return (p * (1.0 / denom)).astype(BF16)


def _sink_col(sinks_ref, kvh, rows, t):
    row_g = lax.broadcasted_iota(jnp.int32, (rows, 1), 0) // t
    col = jnp.zeros((rows, 1), F32)
    for g in range(GROUP):
        col = jnp.where(row_g == g, sinks_ref[kvh * GROUP + g], col)
    return col


def _swa_prompt_kernel(sinks_ref, q_ref, kp_ref, kc_ref, vp_ref, vc_ref, tile_ref, o_ref, *, tq):
    qi = pl.program_id(1)
    rows = GROUP * tq
    col = lax.broadcasted_iota(jnp.int32, (rows, 2 * tq), 1)
    keep = (col >= tq) | (qi > 0)
    vband = jnp.concatenate([vp_ref[...], vc_ref[...]], axis=0)
    for kvh in range(N_KV_HEADS):
        q4 = jnp.concatenate([q_ref[:, _head_cols(kvh * GROUP + g)] for g in range(GROUP)], axis=0)
        kband_t = jnp.concatenate([kp_ref[kvh, :HEAD_DIM], kc_ref[kvh, :HEAD_DIM]], axis=1)
        s = _dot(q4, kband_t) + jnp.where(keep, tile_ref[kvh], NEG_INF)
        o = _dot(_swa_probs(s, _sink_col(sinks_ref, kvh, rows, tq)),
                 vband[:, kvh * LANES:kvh * LANES + HEAD_DIM])
        for g in range(GROUP):
            o_ref[:, _head_cols(kvh * GROUP + g)] = o[g * tq:(g + 1) * tq].astype(BF16)


def swa_prompt(q, kb, vb, sinks, batch, seq):
    tq = WINDOW
    nq = seq // tq
    qi = np.arange(tq)[:, None]
    kj = np.arange(2 * tq)[None, :]
    tile = _swa_tile(tq, tq + qi - kj)
    cur = lambda b, i: (b * nq + i, 0)
    prev = lambda b, i: (b * nq + jnp.maximum(i - 1, 0), 0)
    cur_t = lambda b, i: (b, 0, 0, i)
    prev_t = lambda b, i: (b, 0, 0, jnp.maximum(i - 1, 0))
    return pl.pallas_call(
        functools.partial(_swa_prompt_kernel, tq=tq),
        grid=(batch, nq),
        in_specs=[pl.BlockSpec(memory_space=pltpu.SMEM),
                  pl.BlockSpec((tq, Q_DIM), cur),
                  pl.BlockSpec((None, N_KV_HEADS, LANES, tq), prev_t),
                  pl.BlockSpec((None, N_KV_HEADS, LANES, tq), cur_t),
                  pl.BlockSpec((tq, N_KV_HEADS * LANES), prev),
                  pl.BlockSpec((tq, N_KV_HEADS * LANES), cur),
                  pl.BlockSpec((N_KV_HEADS, GROUP * tq, 2 * tq), lambda b, i: (0, 0, 0))],
        out_specs=pl.BlockSpec((tq, Q_DIM), cur),
        out_shape=jax.ShapeDtypeStruct((batch * seq, Q_DIM), BF16),
        compiler_params=_cparams(("parallel", "parallel")),
        name="swa_prompt",
    )(sinks, q, kb, kb, vb, vb, tile)


def _swa_sample_kernel(sinks_ref, q_ref, kn_ref, vn_ref, kbuf_ref, vbuf_ref, tile_ref,
                       o_ref, ko_ref, vo_ref, *, t, w_buf):
    rows = GROUP * t
    row_pad = jnp.zeros((w_buf - t, HEAD_DIM), F32)
    lane_pad = jnp.zeros((w_buf, LANES - HEAD_DIM), F32)
    sink_cols = [_sink_col(sinks_ref, kvh, rows, t) for kvh in range(N_KV_HEADS)]

    def shifted(buf_t, new):
        new_t = jnp.concatenate([new, lane_pad], axis=1).T[:HEAD_DIM, :t]
        return jnp.concatenate([buf_t[:, t:], new_t], axis=1)

    for s in range(q_ref.shape[0]):
        q = q_ref[s].astype(F32)
        for kvh in range(N_KV_HEADS):
            kbuf_t = kbuf_ref[s, kvh]
            vbuf_t = vbuf_ref[s, kvh]
            kn = jnp.concatenate([kn_ref[s, :, kvh, :], row_pad], axis=0)
            vn = jnp.concatenate([vn_ref[s, :, kvh, :], row_pad], axis=0)
            ko_ref[s, kvh] = shifted(kbuf_t, kn)
            vo_ref[s, kvh] = shifted(vbuf_t, vn)
            q4 = jnp.concatenate(
                [q[:, _head_cols(kvh * GROUP + g)] for g in range(GROUP)], axis=0).astype(BF16)
            sc = jnp.concatenate([_dot(q4, kbuf_t.astype(BF16)), _dot_nt(q4, kn.astype(BF16))], axis=1)
            p = _swa_probs(sc + tile_ref[kvh], sink_cols[kvh])
            o = _dot_nt(p[:, :w_buf], vbuf_t.astype(BF16)) + _dot(p[:, w_buf:], vn.astype(BF16))
            for g in range(GROUP):
                o_ref[s, :, _head_cols(kvh * GROUP + g)] = o[g * t:(g + 1) * t].astype(BF16)


def swa_sample(q_s, k_new, v_new, buf_k, buf_v, layer, sinks):
    db, t, _ = q_s.shape
    w_buf = buf_k.shape[-1]
    ns = SWA_SEQS_PER_STEP
    idx = np.arange(2 * w_buf)[None, :]
    dist = np.where(idx < w_buf + t, (w_buf + np.arange(t))[:, None] - idx, -1)
    tile = _swa_tile(t, dist)
    seq_blk = lambda i: (i, 0, 0)
    seq_blk4 = lambda i: (i, 0, 0, 0)
    buf_blk = lambda i: (layer, i, 0, 0, 0)
    kv_shape = (ns, N_KV_HEADS, HEAD_DIM, w_buf)
    return pl.pallas_call(
        functools.partial(_swa_sample_kernel, t=t, w_buf=w_buf),
        grid=(db // ns,),
        in_specs=[pl.BlockSpec(memory_space=pltpu.SMEM),
                  pl.BlockSpec((ns, t, Q_DIM), seq_blk),
                  pl.BlockSpec((ns, t, N_KV_HEADS, HEAD_DIM), seq_blk4),
                  pl.BlockSpec((ns, t, N_KV_HEADS, HEAD_DIM), seq_blk4),
                  pl.BlockSpec((None,) + kv_shape, buf_blk),
                  pl.BlockSpec((None,) + kv_shape, buf_blk),
                  pl.BlockSpec((N_KV_HEADS, GROUP * t, 2 * w_buf), lambda i: (0, 0, 0))],
        out_specs=[pl.BlockSpec((ns, t, Q_DIM), seq_blk),
                   pl.BlockSpec(kv_shape, seq_blk4),
                   pl.BlockSpec(kv_shape, seq_blk4)],
        out_shape=[jax.ShapeDtypeStruct((db, t, Q_DIM), BF16),
                   jax.ShapeDtypeStruct((db, N_KV_HEADS, HEAD_DIM, w_buf), F32),
                   jax.ShapeDtypeStruct((db, N_KV_HEADS, HEAD_DIM, w_buf), F32)],
        compiler_params=_cparams(("parallel",)),
        name="swa_sample",
    )(sinks, q_s, k_new, v_new, buf_k, buf_v, tile)


def _pad_router(w):
    w = jnp.pad(w, ((0, 0), (0, ROUTER_LANES - w.shape[1])))
    hi = w.astype(BF16)
    return hi, (w - hi.astype(F32)).astype(BF16)


def _token_tile(n_p, n_s):
    for tm in (512, 256, 128, 64, 32, 16, 8):
        if n_p % tm == 0 and n_s % tm == 0:
            return tm
    raise ValueError(f"token counts {n_p}, {n_s} are not multiples of 8")


def kernel(x_prompt, x_sample, cache_sb_k, cache_sb_v, cache_swa_k, cache_swa_v, page_table,
           p_prompt, p_sample, attn_norm, w_qkv, w_o, sb_bias, attn_sinks, ffn_norm, w_gu_dense,
           w_down_dense, w_router, w_gu_expert, w_down_expert, ple_norm, w_ple_proj,
           w_ple_gate, final_norm):
    batch, seq, _ = x_prompt.shape
    db, t, _ = x_sample.shape
    depth = attn_norm.shape[0]
    n_p = batch * seq
    n_s = db * t
    tm = _token_tile(n_p, n_s)
    assert seq % SB_KEY_BLOCK == 0 and seq % WINDOW == 0

    h = jnp.concatenate([x_prompt.reshape(n_p, D_MODEL), x_sample.reshape(n_s, D_MODEL)], axis=0)
    p_all = jnp.concatenate([p_prompt.reshape(depth, n_p, -1), p_sample.reshape(depth, n_s, -1)], axis=1)
    w_keep = min(WINDOW, seq)
    d_ff = w_down_dense.shape[1]
    d_ffe = w_down_expert.shape[2]
    to_t = lambda x: jnp.moveaxis(x, -3, -1)
    from_t = lambda x: jnp.moveaxis(x, -1, -3)
    pool_k, pool_v = to_t(cache_sb_k), to_t(cache_sb_v)
    buf_k, buf_v = to_t(cache_swa_k), to_t(cache_swa_v)

    sb_kp, sb_vp, sb_ks, sb_vs = [], [], [], []
    sw_kp, sw_vp, sw_ks, sw_vs = [], [], [], []
    for i in range(depth):
        j = i // 2
        g_attn = attn_norm[i][None]
        w_i = w_qkv[i].astype(BF16)
        q_p, kt_p, vt_p, kb, vb = qkv_prompt(h, g_attn, w_i, batch, seq, tm)
        q_s, k_s, v_s = qkv_sample(h, g_attn, w_i, n_p, n_s, tm)
        q_s = q_s.reshape(db, t, Q_DIM)
        k_s = k_s.reshape(db, t, N_KV_HEADS, HEAD_DIM)
        v_s = v_s.reshape(db, t, N_KV_HEADS, HEAD_DIM)
        if i % 2 == 0:
            o_p = sb_prompt(q_p, kb, vb, sb_bias[j], batch, seq, SB_QUERY_BLOCK, SB_KEY_BLOCK)
            o_s = sb_sample(q_s, k_s, v_s, pool_k, pool_v, page_table, j, sb_bias[j])
            sb_kp.append(from_t(kt_p)); sb_vp.append(from_t(vt_p)); sb_ks.append(k_s); sb_vs.append(v_s)
        else:
            o_p = swa_prompt(q_p, kb, vb, attn_sinks[j], batch, seq)
            o_s, bks, bvs = swa_sample(q_s, k_s, v_s, buf_k, buf_v, j, attn_sinks[j])
            sw_kp.append(from_t(kt_p[..., seq - w_keep:])); sw_vp.append(from_t(vt_p[..., seq - w_keep:]))
            sw_ks.append(from_t(bks)); sw_vs.append(from_t(bvs))
        o_s = o_s.reshape(n_s, Q_DIM)
        if i % 2 == 0:
            h1, xn = post_attn(h, o_p, o_s, w_o[i].astype(BF16), ffn_norm[i][None], None, tm)
            h2 = dense_ffn(h1, xn, w_gu_dense[j].astype(BF16), w_down_dense[j].astype(BF16), tm, d_ff // 2)
        else:
            h1, xn, route = post_attn(h, o_p, o_s, w_o[i].astype(BF16), ffn_norm[i][None],
                                      _pad_router(w_router[j]), tm)
            h2 = moe_ffn(h1, xn, route, w_gu_expert[j].astype(BF16), w_down_expert[j].astype(BF16),
                         tm, d_ffe // 2)
        h = ple_add(h2, p_all[i], ple_norm[i][None], w_ple_proj[i].astype(BF16),
                    w_ple_gate[i].astype(BF16), final_norm[None] if i == depth - 1 else None, tm)

    y_prompt = h[:n_p].reshape(batch, seq, D_MODEL)
    y_sample = h[n_p:].reshape(db, t, D_MODEL)
    return (y_prompt, y_sample,
            jnp.stack(sb_kp), jnp.stack(sb_vp), jnp.stack(sb_ks), jnp.stack(sb_vs),
            jnp.stack(sw_kp), jnp.stack(sw_vp), jnp.stack(sw_ks), jnp.stack(sw_vs))
```

```python
import functools

import numpy as np
import jax
import jax.numpy as jnp
from jax import lax
from jax.experimental import pallas as pl
from jax.experimental.pallas import tpu as pltpu

F32 = jnp.float32
BF16 = jnp.bfloat16

D_MODEL = 1024
N_HEADS = 16
HEAD_DIM = 64
N_KV_HEADS = 4
GROUP = N_HEADS // N_KV_HEADS
Q_DIM = N_HEADS * HEAD_DIM
KV_DIM = N_KV_HEADS * HEAD_DIM
QKV_DIM = Q_DIM + 2 * KV_DIM
ATTN_SCALE = HEAD_DIM ** -0.5
WINDOW = 128
PAGE_SIZE = 128
N_EXPERTS = 8
TOP_K = 2
RMS_EPS = 1e-6

LANES = 128
ROUTER_LANES = LANES
VMEM_LIMIT = 56 * 1024 * 1024
PAGES_PER_STEP = 32
SWA_SEQS_PER_STEP = 8
SB_KEY_BLOCK = 256
SB_QUERY_BLOCK = 256
SB_ROW_CHUNK = GROUP * SB_QUERY_BLOCK
SUBLANES = 8
NEG_INF = float("-inf")
BIAS_PARTS = 3
LANE_EXPERT = N_EXPERTS
LANE_GATE = N_EXPERTS + TOP_K


def _cparams(sem, vmem=VMEM_LIMIT):
    return pltpu.CompilerParams(dimension_semantics=sem, vmem_limit_bytes=vmem)


def _dot(a, b):
    return jnp.dot(a, b, preferred_element_type=F32)


def _dot_nt(a, b):
    return lax.dot_general(a, b, (((1,), (1,)), ((), ())), preferred_element_type=F32)


def _head_cols(h):
    return slice(h * HEAD_DIM, (h + 1) * HEAD_DIM)


def _rms(x, g):
    ms = jnp.mean(x * x, axis=-1, keepdims=True)
    return x * lax.rsqrt(ms + RMS_EPS) * g


def _neg_abs(z):
    return lax.bitcast_convert_type(lax.bitcast_convert_type(z, jnp.int32) | jnp.int32(-2 ** 31), F32)


def _softplus(z):
    return jnp.maximum(z, 0.0) + jnp.log(1.0 + jnp.exp(_neg_abs(z)))


def _qkv(h_ref, g_ref, w_ref, q_ref):
    xn = _rms(h_ref[...], g_ref[...]).astype(BF16)
    t = _dot(xn, w_ref[...])
    q_ref[...] = (t[:, :Q_DIM] * ATTN_SCALE).astype(BF16)
    return t[:, Q_DIM:Q_DIM + KV_DIM], t[:, Q_DIM + KV_DIM:]


def _qkv_prompt_kernel(h_ref, g_ref, w_ref, q_ref, kt_ref, vt_ref, kb_ref, vb_ref):
    k, v = _qkv(h_ref, g_ref, w_ref, q_ref)
    kt = k.T
    vt = v.T
    row = lax.broadcasted_iota(jnp.int32, (LANES - HEAD_DIM, k.shape[0]), 0)
    ones = jnp.where(row < BIAS_PARTS, 1.0, 0.0).astype(BF16)
    for kvh in range(N_KV_HEADS):
        kt_ref[kvh] = kt[_head_cols(kvh)]
        vt_ref[kvh] = vt[_head_cols(kvh)]
        kb_ref[kvh, :HEAD_DIM] = kt[_head_cols(kvh)].astype(BF16)
        kb_ref[kvh, HEAD_DIM:] = ones
        vb_ref[:, kvh * LANES:kvh * LANES + HEAD_DIM] = v[:, _head_cols(kvh)].astype(BF16)
        vb_ref[:, kvh * LANES + HEAD_DIM:(kvh + 1) * LANES] = jnp.zeros((v.shape[0], LANES - HEAD_DIM), BF16)


def _qkv_sample_kernel(h_ref, g_ref, w_ref, q_ref, k_ref, v_ref):
    k, v = _qkv(h_ref, g_ref, w_ref, q_ref)
    for kvh in range(N_KV_HEADS):
        k_ref[:, kvh, :] = k[:, _head_cols(kvh)]
        v_ref[:, kvh, :] = v[:, _head_cols(kvh)]


def _qkv_in_specs(tm, row):
    return [pl.BlockSpec((tm, D_MODEL), row),
            pl.BlockSpec((1, D_MODEL), lambda *_: (0, 0)),
            pl.BlockSpec((D_MODEL, QKV_DIM), lambda *_: (0, 0))]


def qkv_prompt(h, g, w, batch, seq, tm):
    ns = seq // tm
    row = lambda b, i: (b * ns + i, 0)
    tblk = lambda b, i: (b, 0, 0, i)
    return pl.pallas_call(
        _qkv_prompt_kernel,
        grid=(batch, ns),
        in_specs=_qkv_in_specs(tm, row),
        out_specs=[pl.BlockSpec((tm, Q_DIM), row),
                   pl.BlockSpec((None, N_KV_HEADS, HEAD_DIM, tm), tblk),
                   pl.BlockSpec((None, N_KV_HEADS, HEAD_DIM, tm), tblk),
                   pl.BlockSpec((None, N_KV_HEADS, LANES, tm), tblk),
                   pl.BlockSpec((tm, N_KV_HEADS * LANES), row)],
        out_shape=[jax.ShapeDtypeStruct((batch * seq, Q_DIM), BF16),
                   jax.ShapeDtypeStruct((batch, N_KV_HEADS, HEAD_DIM, seq), F32),
                   jax.ShapeDtypeStruct((batch, N_KV_HEADS, HEAD_DIM, seq), F32),
                   jax.ShapeDtypeStruct((batch, N_KV_HEADS, LANES, seq), BF16),
                   jax.ShapeDtypeStruct((batch * seq, N_KV_HEADS * LANES), BF16)],
        compiler_params=_cparams(("parallel", "parallel")),
        name="qkv_prompt",
    )(h, g, w)


def qkv_sample(h, g, w, first_row, n_rows, tm):
    first = first_row // tm
    row_in = lambda i: (first + i, 0)
    row = lambda i: (i, 0)
    row3 = lambda i: (i, 0, 0)
    return pl.pallas_call(
        _qkv_sample_kernel,
        grid=(n_rows // tm,),
        in_specs=_qkv_in_specs(tm, row_in),
        out_specs=[pl.BlockSpec((tm, Q_DIM), row),
                   pl.BlockSpec((tm, N_KV_HEADS, HEAD_DIM), row3),
                   pl.BlockSpec((tm, N_KV_HEADS, HEAD_DIM), row3)],
        out_shape=[jax.ShapeDtypeStruct((n_rows, Q_DIM), BF16),
                   jax.ShapeDtypeStruct((n_rows, N_KV_HEADS, HEAD_DIM), F32),
                   jax.ShapeDtypeStruct((n_rows, N_KV_HEADS, HEAD_DIM), F32)],
        compiler_params=_cparams(("parallel",)),
        name="qkv_sample",
    )(h, g, w)


def _route(xn, wh_ref, wl_ref):
    xh = xn.astype(BF16)
    xl = (xn - xh.astype(F32)).astype(BF16)
    wh = wh_ref[...]
    logits = _dot(xh, wh) + _dot(xl, wh) + _dot(xh, wl_ref[...])
    lane = lax.broadcasted_iota(jnp.int32, logits.shape, 1)
    logits = jnp.where(lane < N_EXPERTS, logits, NEG_INF)
    m1 = jnp.max(logits, axis=1, keepdims=True)
    i1 = jnp.min(jnp.where(logits == m1, lane, ROUTER_LANES), axis=1, keepdims=True)
    rest = jnp.where(lane == i1, NEG_INF, logits)
    m2 = jnp.max(rest, axis=1, keepdims=True)
    i2 = jnp.min(jnp.where(rest == m2, lane, ROUTER_LANES), axis=1, keepdims=True)
    e2 = jnp.exp(m2 - m1)
    g1 = 1.0 / (1.0 + e2)
    out = jnp.where(lane == LANE_EXPERT, i1.astype(F32), 0.0)
    out = jnp.where(lane == LANE_EXPERT + 1, i2.astype(F32), out)
    out = jnp.where(lane == LANE_GATE, g1, out)
    return jnp.where(lane == LANE_GATE + 1, e2 * g1, out)


def _post_attn_kernel(h_ref, ap_ref, as_ref, wo_ref, g_ref, *rest, routed, prompt_tiles):
    if routed:
        wh_ref, wl_ref, h1_ref, xn_ref, route_ref = rest
    else:
        h1_ref, xn_ref = rest
    attn = jnp.where(pl.program_id(0) < prompt_tiles, ap_ref[...], as_ref[...])
    h1 = h_ref[...] + _dot(attn, wo_ref[...])
    h1_ref[...] = h1
    xn = _rms(h1, g_ref[...])
    xn_ref[...] = xn.astype(xn_ref.dtype)
    if routed:
        route_ref[...] = _route(xn, wh_ref, wl_ref)


def post_attn(h, attn_p, attn_s, wo, g, router, tm):
    n = h.shape[0]
    prompt_tiles = attn_p.shape[0] // tm
    sample_tiles = attn_s.shape[0] // tm
    row = lambda i: (i, 0)
    fixed = lambda i: (0, 0)
    routed = router is not None
    in_specs = [pl.BlockSpec((tm, D_MODEL), row),
                pl.BlockSpec((tm, Q_DIM), lambda i: (jnp.minimum(i, prompt_tiles - 1), 0)),
                pl.BlockSpec((tm, Q_DIM), lambda i: (jnp.clip(i - prompt_tiles, 0, sample_tiles - 1), 0)),
                pl.BlockSpec((Q_DIM, D_MODEL), fixed),
                pl.BlockSpec((1, D_MODEL), fixed)]
    out_specs = [pl.BlockSpec((tm, D_MODEL), row), pl.BlockSpec((tm, D_MODEL), row)]
    out_shape = [jax.ShapeDtypeStruct((n, D_MODEL), F32),
                 jax.ShapeDtypeStruct((n, D_MODEL), F32 if routed else BF16)]
    args = [h, attn_p, attn_s, wo, g]
    if routed:
        in_specs += [pl.BlockSpec((D_MODEL, ROUTER_LANES), fixed)] * 2
        out_specs.append(pl.BlockSpec((tm, ROUTER_LANES), row))
        out_shape.append(jax.ShapeDtypeStruct((n, ROUTER_LANES), F32))
        args += list(router)
    return pl.pallas_call(
        functools.partial(_post_attn_kernel, routed=routed, prompt_tiles=prompt_tiles),
        grid=(n // tm,),
        in_specs=in_specs, out_specs=out_specs, out_shape=out_shape,
        compiler_params=_cparams(("parallel",)),
        name="post_attn_routed" if routed else "post_attn",
    )(*args)


def _swiglu_partial(x, wg, wu, wd):
    g = _dot(x, wg)
    u = _dot(x, wu)
    return _dot((g * jax.nn.sigmoid(g) * u).astype(BF16), wd)


def _ffn_kernel(h_ref, x_ref, wg_ref, wu_ref, wd_ref, o_ref, acc_ref):
    f = pl.program_id(1)

    @pl.when(f == 0)
    def _():
        acc_ref[...] = jnp.zeros_like(acc_ref)

    acc_ref[...] += _swiglu_partial(x_ref[...], wg_ref[...], wu_ref[...], wd_ref[...])

    @pl.when(f == pl.num_programs(1) - 1)
    def _():
        o_ref[...] = h_ref[...] + acc_ref[...]


def dense_ffn(h, xn, w_gu, w_down, tm, tf):
    n = h.shape[0]
    d_ff = w_down.shape[0]
    nf = d_ff // tf
    row = lambda i, f: (i, 0)
    mode = dict(pipeline_mode=pl.Buffered(1)) if nf == 1 else {}
    return pl.pallas_call(
        _ffn_kernel,
        grid=(n // tm, nf),
        in_specs=[pl.BlockSpec((tm, D_MODEL), row),
                  pl.BlockSpec((tm, D_MODEL), row),
                  pl.BlockSpec((D_MODEL, tf), lambda i, f: (0, f), **mode),
                  pl.BlockSpec((D_MODEL, tf), lambda i, f: (0, nf + f), **mode),
                  pl.BlockSpec((tf, D_MODEL), lambda i, f: (f, 0), **mode)],
        out_specs=pl.BlockSpec((tm, D_MODEL), row),
        out_shape=jax.ShapeDtypeStruct((n, D_MODEL), F32),
        scratch_shapes=[pltpu.VMEM((tm, D_MODEL), F32)],
        compiler_params=_cparams(("parallel", "arbitrary")),
        name="dense_ffn",
    )(h, xn, w_gu, w_gu, w_down)


def _ple_kernel(h_ref, p_ref, g_ref, wp_ref, wg_ref, *rest, final):
    if final:
        gf_ref, o_ref = rest
    else:
        (o_ref,) = rest
    h = h_ref[...]
    gate = jax.nn.sigmoid(_dot(_rms(h, g_ref[...]).astype(BF16), wg_ref[...]))
    out = h + _dot(p_ref[...].astype(BF16), wp_ref[...]) * gate
    if final:
        out = _rms(out, gf_ref[...])
    o_ref[...] = out


def ple_add(h, p, g, w_proj, w_gate, final_g, tm, first_row=0, n_rows=None):
    n = h.shape[0] if n_rows is None else n_rows
    first = first_row // tm
    ple_dim = p.shape[1]
    row_in = lambda i: (first + i, 0)
    row = lambda i: (i, 0)
    fixed = lambda i: (0, 0)
    final = final_g is not None
    in_specs = [pl.BlockSpec((tm, D_MODEL), row_in),
                pl.BlockSpec((tm, ple_dim), row_in),
                pl.BlockSpec((1, D_MODEL), fixed),
                pl.BlockSpec((ple_dim, D_MODEL), fixed),
                pl.BlockSpec((D_MODEL, D_MODEL), fixed)]
    args = [h, p, g, w_proj, w_gate]
    if final:
        in_specs.append(pl.BlockSpec((1, D_MODEL), fixed))
        args.append(final_g)
    return pl.pallas_call(
        functools.partial(_ple_kernel, final=final),
        grid=(n // tm,),
        in_specs=in_specs,
        out_specs=pl.BlockSpec((tm, D_MODEL), row),
        out_shape=jax.ShapeDtypeStruct((n, D_MODEL), F32),
        compiler_params=_cparams(("parallel",)),
        name="ple_final" if final else "ple",
    )(*args)


def _row_copy(src_ref, src_row, dst_ref, dst_row, sem):
    return pltpu.make_async_copy(src_ref.at[pl.ds(src_row, 1)], dst_ref.at[pl.ds(dst_row, 1)], sem)


def _for_rows(tm, fn):
    def body(g, c):
        base = pl.multiple_of(g * SUBLANES, SUBLANES)
        for k in range(SUBLANES):
            fn(base + k)
        return c

    lax.fori_loop(0, tm // SUBLANES, body, 0)


def _moe_dispatch_kernel(pos_ref, x_ref, xs_in_ref, xs_ref, sem, *, tm):
    del xs_in_ref

    def copies(r):
        return [_row_copy(x_ref, r, xs_ref, pos_ref[s, r], sem) for s in range(TOP_K)]

    def start(r):
        for cp in copies(r):
            cp.start()

    def wait(r):
        for cp in copies(r):
            cp.wait()

    _for_rows(tm, start)
    _for_rows(tm, wait)


def moe_dispatch(xn, pos_tiles, n_rows, tm):
    n = xn.shape[0]
    xs0 = jnp.zeros((n_rows, D_MODEL), F32)
    return pl.pallas_call(
        functools.partial(_moe_dispatch_kernel, tm=tm),
        grid=(n // tm,),
        in_specs=[pl.BlockSpec((None, TOP_K, tm), lambda i: (i, 0, 0), memory_space=pltpu.SMEM),
                  pl.BlockSpec((tm, D_MODEL), lambda i: (i, 0)),
                  pl.BlockSpec(memory_space=pl.ANY)],
        out_specs=pl.BlockSpec(memory_space=pl.ANY),
        out_shape=jax.ShapeDtypeStruct((n_rows, D_MODEL), F32),
        scratch_shapes=[pltpu.SemaphoreType.DMA(())],
        input_output_aliases={2: 0},
        compiler_params=_cparams(("arbitrary",)),
        name="moe_dispatch",
    )(pos_tiles, xn, xs0)


def _moe_grouped_kernel(te_ref, nt_ref, x_ref, wg_ref, wu_ref, wd_ref, o_ref, acc_ref):
    i = pl.program_id(0)
    f = pl.program_id(1)
    live = i < nt_ref[0]

    @pl.when(live & (f == 0))
    def _():
        acc_ref[...] = jnp.zeros_like(acc_ref)

    @pl.when(live)
    def _():
        acc_ref[...] += _swiglu_partial(x_ref[...].astype(BF16), wg_ref[0], wu_ref[0], wd_ref[0])

    @pl.when(live & (f == pl.num_programs(1) - 1))
    def _():
        o_ref[...] = acc_ref[...]

    @pl.when(jnp.logical_not(live) & (f == pl.num_programs(1) - 1))
    def _():
        o_ref[...] = jnp.zeros_like(o_ref)


def moe_grouped(xs, tile_expert, n_tiles, w_gu, w_down, tm, tf):
    n_rows = xs.shape[0]
    max_tiles = n_rows // tm
    d_ff = w_down.shape[1]
    nf = d_ff // tf
    last = lambda i, nt: jnp.minimum(i, nt[0] - 1)
    fblk = lambda i, f, nt: jnp.where(i < nt[0], f, nf - 1)
    row = lambda i, f, te, nt: (last(i, nt), 0)
    grid_spec = pltpu.PrefetchScalarGridSpec(
        num_scalar_prefetch=2,
        grid=(max_tiles, nf),
        in_specs=[pl.BlockSpec((tm, D_MODEL), row),
                  pl.BlockSpec((1, D_MODEL, tf), lambda i, f, te, nt: (te[last(i, nt)], 0, fblk(i, f, nt))),
                  pl.BlockSpec((1, D_MODEL, tf), lambda i, f, te, nt: (te[last(i, nt)], 0, nf + fblk(i, f, nt))),
                  pl.BlockSpec((1, tf, D_MODEL), lambda i, f, te, nt: (te[last(i, nt)], fblk(i, f, nt), 0))],
        out_specs=pl.BlockSpec((tm, D_MODEL), lambda i, f, te, nt: (i, 0)),
        scratch_shapes=[pltpu.VMEM((tm, D_MODEL), F32)])
    return pl.pallas_call(
        _moe_grouped_kernel,
        grid_spec=grid_spec,
        out_shape=jax.ShapeDtypeStruct((n_rows, D_MODEL), F32),
        compiler_params=_cparams(("arbitrary", "arbitrary")),
        name="moe_grouped",
    )(tile_expert, n_tiles, xs, w_gu, w_gu, w_down)


def _moe_combine_kernel(pos_ref, h_ref, route_ref, ys_ref, o_ref, ybuf_ref, sem, *, tm):
    def copies(r):
        return [_row_copy(ys_ref, pos_ref[s, r], ybuf_ref.at[s], r, sem) for s in range(TOP_K)]

    def start(r):
        for cp in copies(r):
            cp.start()

    def wait(r):
        for cp in copies(r):
            cp.wait()

    _for_rows(tm, start)
    route = route_ref[...]
    lane = lax.broadcasted_iota(jnp.int32, route.shape, 1)
    gates = [jnp.sum(jnp.where(lane == LANE_GATE + s, route, 0.0), axis=1, keepdims=True)
             for s in range(TOP_K)]
    _for_rows(tm, wait)
    out = gates[0] * ybuf_ref[0]
    for s in range(1, TOP_K):
        out = out + gates[s] * ybuf_ref[s]
    o_ref[...] = h_ref[...] + out


def moe_combine(h, route, ys, pos_tiles, tm):
    n = h.shape[0]
    row = lambda i: (i, 0)
    return pl.pallas_call(
        functools.partial(_moe_combine_kernel, tm=tm),
        grid=(n // tm,),
        in_specs=[pl.BlockSpec((None, TOP_K, tm), lambda i: (i, 0, 0), memory_space=pltpu.SMEM),
                  pl.BlockSpec((tm, D_MODEL), row),
                  pl.BlockSpec((tm, ROUTER_LANES), row),
                  pl.BlockSpec(memory_space=pl.ANY)],
        out_specs=pl.BlockSpec((tm, D_MODEL), row),
        out_shape=jax.ShapeDtypeStruct((n, D_MODEL), F32),
        scratch_shapes=[pltpu.VMEM((TOP_K, tm, D_MODEL), F32), pltpu.SemaphoreType.DMA(())],
        compiler_params=_cparams(("arbitrary",)),
        name="moe_combine",
    )(pos_tiles, h, route, ys)


def _routing_tables(route, tm):
    n = route.shape[0]
    max_tiles = -(-(TOP_K * n + N_EXPERTS * (tm - 1)) // tm)
    expert = jnp.concatenate([route[:, LANE_EXPERT + s] for s in range(TOP_K)]).astype(jnp.int32)
    onehot = (expert[:, None] == jnp.arange(N_EXPERTS, dtype=jnp.int32)[None, :]).astype(jnp.int32)
    csum = jnp.cumsum(onehot, axis=0)
    counts = csum[-1]
    tiles = (counts + tm - 1) // tm
    tile_end = jnp.cumsum(tiles)
    row_start = (tile_end - tiles) * tm
    pos = jnp.sum(onehot * (csum - 1 + row_start[None, :]), axis=1)
    pos_tiles = pos.reshape(TOP_K, n // tm, tm).transpose(1, 0, 2)
    tile_expert = jnp.sum(jnp.arange(max_tiles, dtype=jnp.int32)[:, None] >= tile_end[None, :], axis=1)
    tile_expert = jnp.minimum(tile_expert, N_EXPERTS - 1).astype(jnp.int32)
    return pos_tiles, tile_expert, tile_end[-1:].astype(jnp.int32), max_tiles * tm


def moe_ffn(h, xn, route, w_gu, w_down, tm, tf):
    pos_tiles, tile_expert, n_tiles, n_rows = _routing_tables(route, tm)
    xs = moe_dispatch(xn, pos_tiles, n_rows, tm)
    ys = moe_grouped(xs, tile_expert, n_tiles, w_gu, w_down, tm, tf)
    return moe_combine(h, route, ys, pos_tiles, tm)


def _later_keys_matrix(tk, with_row_sum):
    j = np.arange(tk)[:, None]
    s = np.arange(2 * tk if with_row_sum else tk)[None, :]
    return jnp.asarray(np.where(s < tk, j > s, True), dtype=BF16)


def _split_bias(bias):
    parts = []
    rest = bias.astype(F32)
    for _ in range(BIAS_PARTS):
        p = rest.astype(BF16).astype(F32)
        parts.append(p)
        rest = rest - p
    return jnp.stack(parts)


def _sb_prompt_kernel(bias_ref, q_ref, k_ref, v_ref, m_ref, o_ref, q4_ref, acc_ref, car_ref, *, tq, tk):
    qi = pl.program_id(1)
    rows = GROUP * tq
    row_g = lax.broadcasted_iota(jnp.int32, (rows, HEAD_DIM), 0) // tq
    lane = lax.broadcasted_iota(jnp.int32, (rows, HEAD_DIM), 1)
    for kvh in range(N_KV_HEADS):
        q4 = jnp.concatenate([q_ref[:, _head_cols(kvh * GROUP + g)] for g in range(GROUP)], axis=0)
        bcols = jnp.zeros((rows, HEAD_DIM), F32)
        for g in range(GROUP):
            for part in range(BIAS_PARTS):
                bcols = jnp.where((row_g == g) & (lane == part), bias_ref[part, kvh * GROUP + g], bcols)
        q4_ref[kvh] = jnp.concatenate([q4, bcols.astype(BF16)], axis=1)
    acc_ref[...] = jnp.zeros_like(acc_ref)
    car_ref[...] = jnp.zeros_like(car_ref)

    def block(j, mask):
        start = pl.multiple_of(j * tk, tk)
        later = m_ref[...]
        for kvh in range(N_KV_HEADS):
            k_t = k_ref[kvh, :, pl.ds(start, tk)]
            v = v_ref[pl.ds(start, tk), kvh * LANES:(kvh + 1) * LANES]
            for r0 in range(0, rows, SB_ROW_CHUNK):
                rs = slice(r0, r0 + SB_ROW_CHUNK)
                z = _dot(q4_ref[kvh, rs], k_t)
                sp = _softplus(z)
                ls = z - sp
                if mask is not None:
                    sp = jnp.where(mask[rs], sp, 0.0)
                sp16 = sp.astype(BF16)
                tail = _dot(sp16, later)
                a = jnp.exp(ls - tail - car_ref[kvh, rs])
                if mask is not None:
                    a = jnp.where(mask[rs], a, 0.0)
                acc_ref[kvh, rs] += _dot(a.astype(BF16), v)
                car_ref[kvh, rs] += tail[:, :1] + sp16[:, :1].astype(F32)

    jd = (qi * tq) // tk
    t_pos = qi * tq + lax.broadcasted_iota(jnp.int32, (rows, tk), 0) % tq
    s_pos = jd * tk + lax.broadcasted_iota(jnp.int32, (rows, tk), 1)
    block(jd, s_pos < t_pos)

    def body(i, c):
        block(jd - 1 - i, None)
        return c

    lax.fori_loop(0, jd, body, 0)

    for kvh in range(N_KV_HEADS):
        for g in range(GROUP):
            o_ref[:, _head_cols(kvh * GROUP + g)] = acc_ref[kvh, g * tq:(g + 1) * tq, :HEAD_DIM].astype(BF16)


def sb_prompt(q, kb, vb, bias, batch, seq, tq, tk):
    nq = seq // tq
    rows = GROUP * tq
    return pl.pallas_call(
        functools.partial(_sb_prompt_kernel, tq=tq, tk=tk),
        grid=(batch, nq),
        in_specs=[pl.BlockSpec(memory_space=pltpu.SMEM),
                  pl.BlockSpec((tq, Q_DIM), lambda b, i: (b * nq + i, 0)),
                  pl.BlockSpec((None, N_KV_HEADS, LANES, seq), lambda b, i: (b, 0, 0, 0)),
                  pl.BlockSpec((seq, N_KV_HEADS * LANES), lambda b, i: (b, 0)),
                  pl.BlockSpec((tk, tk), lambda b, i: (0, 0))],
        out_specs=pl.BlockSpec((tq, Q_DIM), lambda b, i: (b * nq + i, 0)),
        out_shape=jax.ShapeDtypeStruct((batch * seq, Q_DIM), BF16),
        scratch_shapes=[pltpu.VMEM((N_KV_HEADS, rows, LANES), BF16),
                        pltpu.VMEM((N_KV_HEADS, rows, LANES), F32),
                        pltpu.VMEM((N_KV_HEADS, rows, 1), F32)],
        compiler_params=_cparams(("parallel", "parallel")),
        name="sb_prompt",
    )(_split_bias(bias), q, kb, vb, _later_keys_matrix(tk, False))


def _sb_sample_kernel(pt_ref, bias_ref, q_ref, kn_ref, vn_ref, m_ref, *rest, t, npg):
    k_refs = rest[:npg]
    v_refs = rest[npg:2 * npg]
    o_ref = rest[2 * npg]
    q4_ref, bt_ref, acc_ref, car_ref = rest[2 * npg + 1:]
    c = pl.program_id(1)
    rows = GROUP * t
    hrows = N_KV_HEADS * rows
    later = m_ref[...]
    heads = range(N_KV_HEADS)

    @pl.when(c == 0)
    def _():
        q = q_ref[...].astype(F32)
        row_h = lax.broadcasted_iota(jnp.int32, (hrows, PAGE_SIZE), 0) // t
        bt = jnp.zeros((hrows, PAGE_SIZE), F32)
        for h in range(N_HEADS):
            bt = jnp.where(row_h == h, bias_ref[h], bt)
        bt_ref[...] = bt
        for kvh in heads:
            q4_ref[kvh] = jnp.concatenate(
                [q[:, _head_cols(kvh * GROUP + g)] for g in range(GROUP)], axis=0).astype(BF16)
        acc_ref[...] = jnp.zeros_like(acc_ref)
        car_ref[...] = jnp.zeros_like(car_ref)

    t_in = lax.broadcasted_iota(jnp.int32, (hrows, PAGE_SIZE), 0) % t
    s_in = lax.broadcasted_iota(jnp.int32, (hrows, PAGE_SIZE), 1)
    own = (s_in < t_in) & (c == 0)
    pad = jnp.zeros((PAGE_SIZE - t, HEAD_DIM), F32)

    def own_rows(ref, kvh):
        return jnp.concatenate([ref[:, kvh, :], pad], axis=0).astype(BF16)

    def pages_cat(refs, kvh):
        return jnp.concatenate([r[kvh].astype(BF16) for r in refs], axis=1)

    def own_masked(x):
        return jnp.concatenate([jnp.where(own, x[:hrows], 0.0), x[hrows:]], axis=0)

    bt = bt_ref[...]
    z_own = jnp.concatenate([_dot_nt(q4_ref[kvh], own_rows(kn_ref, kvh)) for kvh in heads], axis=0) + bt
    zc = [_dot(q4_ref[kvh], pages_cat(k_refs, kvh)) for kvh in heads]
    z = jnp.concatenate(
        [z_own] +
        [jnp.concatenate([zc[kvh][:, i * PAGE_SIZE:(i + 1) * PAGE_SIZE] for kvh in heads], axis=0) + bt
         for i in range(npg)], axis=0)
    sp = own_masked(_softplus(z))
    res = _dot(sp.astype(BF16), later)
    inc = res[:, PAGE_SIZE:]
    carry = car_ref[...]
    carries = []
    for i in range(npg + 1):
        carries.append(carry)
        carry = carry + inc[i * hrows:(i + 1) * hrows]
    car_ref[...] = carry
    a = own_masked(jnp.exp(z - sp - res[:, :PAGE_SIZE] - jnp.concatenate(carries, axis=0))).astype(BF16)
    for kvh in heads:
        lo = kvh * rows
        a_cat = jnp.concatenate(
            [a[(i + 1) * hrows + lo:(i + 1) * hrows + lo + rows] for i in range(npg)], axis=1)
        acc_ref[kvh] += (_dot(a[lo:lo + rows], own_rows(vn_ref, kvh))
                         + _dot_nt(a_cat, pages_cat(v_refs, kvh)))

    @pl.when(c == pl.num_programs(1) - 1)
    def _():
        for kvh in heads:
            for g in range(GROUP):
                o_ref[:, _head_cols(kvh * GROUP + g)] = acc_ref[kvh, g * t:(g + 1) * t, :].astype(BF16)


def sb_sample(q_s, k_new, v_new, pool_k, pool_v, page_table, layer, bias):
    db, t, _ = q_s.shape
    n_pages = page_table.shape[1]
    npg = PAGES_PER_STEP
    assert n_pages % npg == 0, (n_pages, npg)
    nc = n_pages // npg
    rows = GROUP * t
    seq_blk = lambda b, c, pt: (b, 0, 0)
    seq_blk4 = lambda b, c, pt: (b, 0, 0, 0)

    def page_spec(i):
        return pl.BlockSpec(
            (None, None, N_KV_HEADS, HEAD_DIM, PAGE_SIZE),
            lambda b, c, pt: (layer, pt[b * n_pages + (n_pages - 1 - (c * npg + i))], 0, 0, 0))

    grid_spec = pltpu.PrefetchScalarGridSpec(
        num_scalar_prefetch=1,
        grid=(db, nc),
        in_specs=[pl.BlockSpec(memory_space=pltpu.SMEM),
                  pl.BlockSpec((None, t, Q_DIM), seq_blk),
                  pl.BlockSpec((None, t, N_KV_HEADS, HEAD_DIM), seq_blk4),
                  pl.BlockSpec((None, t, N_KV_HEADS, HEAD_DIM), seq_blk4),
                  pl.BlockSpec((PAGE_SIZE, 2 * PAGE_SIZE), lambda b, c, pt: (0, 0))]
                 + [page_spec(i) for i in range(npg)] * 2,
        out_specs=pl.BlockSpec((None, t, Q_DIM), seq_blk),
        scratch_shapes=[pltpu.VMEM((N_KV_HEADS, rows, HEAD_DIM), BF16),
                        pltpu.VMEM((N_KV_HEADS * rows, PAGE_SIZE), F32),
                        pltpu.VMEM((N_KV_HEADS, rows, HEAD_DIM), F32),
                        pltpu.VMEM((N_KV_HEADS * rows, PAGE_SIZE), F32)])
    return pl.pallas_call(
        functools.partial(_sb_sample_kernel, t=t, npg=npg),
        grid_spec=grid_spec,
        out_shape=jax.ShapeDtypeStruct((db, t, Q_DIM), BF16),
        compiler_params=_cparams(("parallel", "arbitrary")),
        name="sb_sample",
    )(page_table.reshape(-1), bias, q_s, k_new, v_new, _later_keys_matrix(PAGE_SIZE, True),
      *([pool_k] * npg), *([pool_v] * npg))


def _alibi_slopes():
    h = np.arange(1, N_HEADS + 1, dtype=np.float64)
    return np.exp2(-8.0 * h / N_HEADS).reshape(N_KV_HEADS, GROUP)


def _swa_tile(t_q, dist):
    valid = (dist >= 0) & (dist < WINDOW)
    tile = np.where(valid[None, None], -_alibi_slopes()[:, :, None, None] * dist[None, None], -np.inf)
    return jnp.asarray(tile.reshape(N_KV_HEADS, GROUP * t_q, dist.shape[1]), dtype=F32)


def _swa_probs(s, sink):
    m = jnp.maximum(jnp.max(s, axis=1, keepdims=True), sink)
    p = jnp.exp(s - m)
    denom = jnp.sum(p, axis=1, keepdims=True) + jnp.exp(sink - m)
    return (p * (1.0 / denom)).astype(BF16)


def _sink_col(sinks_ref, kvh, rows, t):
    row_g = lax.broadcasted_iota(jnp.int32, (rows, 1), 0) // t
    col = jnp.zeros((rows, 1), F32)
    for g in range(GROUP):
        col = jnp.where(row_g == g, sinks_ref[kvh * GROUP + g], col)
    return col


def _swa_prompt_kernel(sinks_ref, q_ref, kp_ref, kc_ref, vp_ref, vc_ref, tile_ref, o_ref, *, tq):
    qi = pl.program_id(1)
    rows = GROUP * tq
    col = lax.broadcasted_iota(jnp.int32, (rows, 2 * tq), 1)
    keep = (col >= tq) | (qi > 0)
    vband = jnp.concatenate([vp_ref[...], vc_ref[...]], axis=0)
    for kvh in range(N_KV_HEADS):
        q4 = jnp.concatenate([q_ref[:, _head_cols(kvh * GROUP + g)] for g in range(GROUP)], axis=0)
        kband_t = jnp.concatenate([kp_ref[kvh, :HEAD_DIM], kc_ref[kvh, :HEAD_DIM]], axis=1)
        s = _dot(q4, kband_t) + jnp.where(keep, tile_ref[kvh], NEG_INF)
        o = _dot(_swa_probs(s, _sink_col(sinks_ref, kvh, rows, tq)),
                 vband[:, kvh * LANES:kvh * LANES + HEAD_DIM])
        for g in range(GROUP):
            o_ref[:, _head_cols(kvh * GROUP + g)] = o[g * tq:(g + 1) * tq].astype(BF16)


def swa_prompt(q, kb, vb, sinks, batch, seq):
    tq = WINDOW
    nq = seq // tq
    qi = np.arange(tq)[:, None]
    kj = np.arange(2 * tq)[None, :]
    tile = _swa_tile(tq, tq + qi - kj)
    cur = lambda b, i: (b * nq + i, 0)
    prev = lambda b, i: (b * nq + jnp.maximum(i - 1, 0), 0)
    cur_t = lambda b, i: (b, 0, 0, i)
    prev_t = lambda b, i: (b, 0, 0, jnp.maximum(i - 1, 0))
    return pl.pallas_call(
        functools.partial(_swa_prompt_kernel, tq=tq),
        grid=(batch, nq),
        in_specs=[pl.BlockSpec(memory_space=pltpu.SMEM),
                  pl.BlockSpec((tq, Q_DIM), cur),
                  pl.BlockSpec((None, N_KV_HEADS, LANES, tq), prev_t),
                  pl.BlockSpec((None, N_KV_HEADS, LANES, tq), cur_t),
                  pl.BlockSpec((tq, N_KV_HEADS * LANES), prev),
                  pl.BlockSpec((tq, N_KV_HEADS * LANES), cur),
                  pl.BlockSpec((N_KV_HEADS, GROUP * tq, 2 * tq), lambda b, i: (0, 0, 0))],
        out_specs=pl.BlockSpec((tq, Q_DIM), cur),
        out_shape=jax.ShapeDtypeStruct((batch * seq, Q_DIM), BF16),
        compiler_params=_cparams(("parallel", "parallel")),
        name="swa_prompt",
    )(sinks, q, kb, kb, vb, vb, tile)


def _swa_sample_kernel(sinks_ref, q_ref, kn_ref, vn_ref, kbuf_ref, vbuf_ref, tile_ref,
                       o_ref, ko_ref, vo_ref, *, t, w_buf):
    rows = GROUP * t
    row_pad = jnp.zeros((w_buf - t, HEAD_DIM), F32)
    lane_pad = jnp.zeros((w_buf, LANES - HEAD_DIM), F32)
    sink_cols = [_sink_col(sinks_ref, kvh, rows, t) for kvh in range(N_KV_HEADS)]

    def shifted(buf_t, new):
        new_t = jnp.concatenate([new, lane_pad], axis=1).T[:HEAD_DIM, :t]
        return jnp.concatenate([buf_t[:, t:], new_t], axis=1)

    for s in range(q_ref.shape[0]):
        q = q_ref[s].astype(F32)
        for kvh in range(N_KV_HEADS):
            kbuf_t = kbuf_ref[s, kvh]
            vbuf_t = vbuf_ref[s, kvh]
            kn = jnp.concatenate([kn_ref[s, :, kvh, :], row_pad], axis=0)
            vn = jnp.concatenate([vn_ref[s, :, kvh, :], row_pad], axis=0)
            ko_ref[s, kvh] = shifted(kbuf_t, kn)
            vo_ref[s, kvh] = shifted(vbuf_t, vn)
            q4 = jnp.concatenate(
                [q[:, _head_cols(kvh * GROUP + g)] for g in range(GROUP)], axis=0).astype(BF16)
            sc = jnp.concatenate([_dot(q4, kbuf_t.astype(BF16)), _dot_nt(q4, kn.astype(BF16))], axis=1)
            p = _swa_probs(sc + tile_ref[kvh], sink_cols[kvh])
            o = _dot_nt(p[:, :w_buf], vbuf_t.astype(BF16)) + _dot(p[:, w_buf:], vn.astype(BF16))
            for g in range(GROUP):
                o_ref[s, :, _head_cols(kvh * GROUP + g)] = o[g * t:(g + 1) * t].astype(BF16)


def swa_sample(q_s, k_new, v_new, buf_k, buf_v, layer, sinks):
    db, t, _ = q_s.shape
    w_buf = buf_k.shape[-1]
    ns = SWA_SEQS_PER_STEP
    idx = np.arange(2 * w_buf)[None, :]
    dist = np.where(idx < w_buf + t, (w_buf + np.arange(t))[:, None] - idx, -1)
    tile = _swa_tile(t, dist)
    seq_blk = lambda i: (i, 0, 0)
    seq_blk4 = lambda i: (i, 0, 0, 0)
    buf_blk = lambda i: (layer, i, 0, 0, 0)
    kv_shape = (ns, N_KV_HEADS, HEAD_DIM, w_buf)
    return pl.pallas_call(
        functools.partial(_swa_sample_kernel, t=t, w_buf=w_buf),
        grid=(db // ns,),
        in_specs=[pl.BlockSpec(memory_space=pltpu.SMEM),
                  pl.BlockSpec((ns, t, Q_DIM), seq_blk),
                  pl.BlockSpec((ns, t, N_KV_HEADS, HEAD_DIM), seq_blk4),
                  pl.BlockSpec((ns, t, N_KV_HEADS, HEAD_DIM), seq_blk4),
                  pl.BlockSpec((None,) + kv_shape, buf_blk),
                  pl.BlockSpec((None,) + kv_shape, buf_blk),
                  pl.BlockSpec((N_KV_HEADS, GROUP * t, 2 * w_buf), lambda i: (0, 0, 0))],
        out_specs=[pl.BlockSpec((ns, t, Q_DIM), seq_blk),
                   pl.BlockSpec(kv_shape, seq_blk4),
                   pl.BlockSpec(kv_shape, seq_blk4)],
        out_shape=[jax.ShapeDtypeStruct((db, t, Q_DIM), BF16),
                   jax.ShapeDtypeStruct((db, N_KV_HEADS, HEAD_DIM, w_buf), F32),
                   jax.ShapeDtypeStruct((db, N_KV_HEADS, HEAD_DIM, w_buf), F32)],
        compiler_params=_cparams(("parallel",)),
        name="swa_sample",
    )(sinks, q_s, k_new, v_new, buf_k, buf_v, tile)


def _pad_router(w):
    w = jnp.pad(w, ((0, 0), (0, ROUTER_LANES - w.shape[1])))
    hi = w.astype(BF16)
    return hi, (w - hi.astype(F32)).astype(BF16)


def _token_tile(n_p, n_s):
    for tm in (512, 256, 128, 64, 32, 16, 8):
        if n_p % tm == 0 and n_s % tm == 0:
            return tm
    raise ValueError(f"token counts {n_p}, {n_s} are not multiples of 8")


def kernel(x_prompt, x_sample, cache_sb_k, cache_sb_v, cache_swa_k, cache_swa_v, page_table,
           p_prompt, p_sample, attn_norm, w_qkv, w_o, sb_bias, attn_sinks, ffn_norm, w_gu_dense,
           w_down_dense, w_router, w_gu_expert, w_down_expert, ple_norm, w_ple_proj,
           w_ple_gate, final_norm):
    batch, seq, _ = x_prompt.shape
    db, t, _ = x_sample.shape
    depth = attn_norm.shape[0]
    n_p = batch * seq
    n_s = db * t
    tm = _token_tile(n_p, n_s)
    assert seq % SB_KEY_BLOCK == 0 and seq % WINDOW == 0

    h = jnp.concatenate([x_prompt.reshape(n_p, D_MODEL), x_sample.reshape(n_s, D_MODEL)], axis=0)
    p_all = jnp.concatenate([p_prompt.reshape(depth, n_p, -1), p_sample.reshape(depth, n_s, -1)], axis=1)
    w_keep = min(WINDOW, seq)
    d_ff = w_down_dense.shape[1]
    d_ffe = w_down_expert.shape[2]
    to_t = lambda x: jnp.moveaxis(x, -3, -1)
    from_t = lambda x: jnp.moveaxis(x, -1, -3)
    pool_k, pool_v = to_t(cache_sb_k), to_t(cache_sb_v)
    buf_k, buf_v = to_t(cache_swa_k), to_t(cache_swa_v)

    sb_kp, sb_vp, sb_ks, sb_vs = [], [], [], []
    sw_kp, sw_vp, sw_ks, sw_vs = [], [], [], []
    for i in range(depth):
        j = i // 2
        g_attn = attn_norm[i][None]
        w_i = w_qkv[i].astype(BF16)
        q_p, kt_p, vt_p, kb, vb = qkv_prompt(h, g_attn, w_i, batch, seq, tm)
        q_s, k_s, v_s = qkv_sample(h, g_attn, w_i, n_p, n_s, tm)
        q_s = q_s.reshape(db, t, Q_DIM)
        k_s = k_s.reshape(db, t, N_KV_HEADS, HEAD_DIM)
        v_s = v_s.reshape(db, t, N_KV_HEADS, HEAD_DIM)
        if i % 2 == 0:
            o_p = sb_prompt(q_p, kb, vb, sb_bias[j], batch, seq, SB_QUERY_BLOCK, SB_KEY_BLOCK)
            o_s = sb_sample(q_s, k_s, v_s, pool_k, pool_v, page_table, j, sb_bias[j])
            sb_kp.append(from_t(kt_p)); sb_vp.append(from_t(vt_p)); sb_ks.append(k_s); sb_vs.append(v_s)
        else:
            o_p = swa_prompt(q_p, kb, vb, attn_sinks[j], batch, seq)
            o_s, bks, bvs = swa_sample(q_s, k_s, v_s, buf_k, buf_v, j, attn_sinks[j])
            sw_kp.append(from_t(kt_p[..., seq - w_keep:])); sw_vp.append(from_t(vt_p[..., seq - w_keep:]))
            sw_ks.append(from_t(bks)); sw_vs.append(from_t(bvs))
        o_s = o_s.reshape(n_s, Q_DIM)
        if i % 2 == 0:
            h1, xn = post_attn(h, o_p, o_s, w_o[i].astype(BF16), ffn_norm[i][None], None, tm)
            h2 = dense_ffn(h1, xn, w_gu_dense[j].astype(BF16), w_down_dense[j].astype(BF16), tm, d_ff)
        else:
            h1, xn, route = post_attn(h, o_p, o_s, w_o[i].astype(BF16), ffn_norm[i][None],
                                      _pad_router(w_router[j]), tm)
            h2 = moe_ffn(h1, xn, route, w_gu_expert[j].astype(BF16), w_down_expert[j].astype(BF16),
                         tm, d_ffe // 2)
        ple_args = (h2, p_all[i], ple_norm[i][None], w_ple_proj[i].astype(BF16), w_ple_gate[i].astype(BF16))
        if i < depth - 1:
            h = ple_add(*ple_args, None, tm)
        else:
            y_p = ple_add(*ple_args, final_norm[None], tm, 0, n_p)
            y_s = ple_add(*ple_args, final_norm[None], tm, n_p, n_s)

    y_prompt = y_p.reshape(batch, seq, D_MODEL)
    y_sample = y_s.reshape(db, t, D_MODEL)
    return (y_prompt, y_sample,
            jnp.stack(sb_kp), jnp.stack(sb_vp), jnp.stack(sb_ks), jnp.stack(sb_vs),
            jnp.stack(sw_kp), jnp.stack(sw_vp), jnp.stack(sw_ks), jnp.stack(sw_vs))
```

```python
import functools

import numpy as np
import jax
import jax.numpy as jnp
from jax import lax
from jax.experimental import pallas as pl
from jax.experimental.pallas import tpu as pltpu

F32 = jnp.float32
BF16 = jnp.bfloat16

D_MODEL = 1024
N_HEADS = 16
HEAD_DIM = 64
N_KV_HEADS = 4
GROUP = N_HEADS // N_KV_HEADS
Q_DIM = N_HEADS * HEAD_DIM
KV_DIM = N_KV_HEADS * HEAD_DIM
QKV_DIM = Q_DIM + 2 * KV_DIM
ATTN_SCALE = HEAD_DIM ** -0.5
WINDOW = 128
PAGE_SIZE = 128
N_EXPERTS = 8
TOP_K = 2
RMS_EPS = 1e-6

LANES = 128
ROUTER_LANES = LANES
VMEM_LIMIT = 56 * 1024 * 1024
PAGES_PER_STEP = 32
SWA_SEQS_PER_STEP = 8
SB_KEY_BLOCK = 256
SB_QUERY_BLOCK = 256
SB_ROW_CHUNK = GROUP * SB_QUERY_BLOCK
SUBLANES = 8
NEG_INF = float("-inf")
BIAS_PARTS = 3
LANE_EXPERT = N_EXPERTS
LANE_GATE = N_EXPERTS + TOP_K


def _cparams(sem, vmem=VMEM_LIMIT):
    return pltpu.CompilerParams(dimension_semantics=sem, vmem_limit_bytes=vmem)


def _dot(a, b):
    return jnp.dot(a, b, preferred_element_type=F32)


def _dot_nt(a, b):
    return lax.dot_general(a, b, (((1,), (1,)), ((), ())), preferred_element_type=F32)


def _head_cols(h):
    return slice(h * HEAD_DIM, (h + 1) * HEAD_DIM)


def _rms(x, g):
    ms = jnp.mean(x * x, axis=-1, keepdims=True)
    return x * lax.rsqrt(ms + RMS_EPS) * g


def _neg_abs(z):
    return lax.bitcast_convert_type(lax.bitcast_convert_type(z, jnp.int32) | jnp.int32(-2 ** 31), F32)


def _softplus(z):
    return jnp.maximum(z, 0.0) + jnp.log(1.0 + jnp.exp(_neg_abs(z)))


def _qkv(h_ref, g_ref, w_ref, q_ref):
    xn = _rms(h_ref[...], g_ref[...]).astype(BF16)
    t = _dot(xn, w_ref[...])
    q_ref[...] = (t[:, :Q_DIM] * ATTN_SCALE).astype(BF16)
    return t[:, Q_DIM:Q_DIM + KV_DIM], t[:, Q_DIM + KV_DIM:]


def _qkv_prompt_kernel(h_ref, g_ref, w_ref, q_ref, kt_ref, vt_ref, kb_ref, vb_ref):
    k, v = _qkv(h_ref, g_ref, w_ref, q_ref)
    kt = k.T
    vt = v.T
    row = lax.broadcasted_iota(jnp.int32, (LANES - HEAD_DIM, k.shape[0]), 0)
    ones = jnp.where(row < BIAS_PARTS, 1.0, 0.0).astype(BF16)
    for kvh in range(N_KV_HEADS):
        kt_ref[kvh] = kt[_head_cols(kvh)]
        vt_ref[kvh] = vt[_head_cols(kvh)]
        kb_ref[kvh, :HEAD_DIM] = kt[_head_cols(kvh)].astype(BF16)
        kb_ref[kvh, HEAD_DIM:] = ones
        vb_ref[:, kvh * LANES:kvh * LANES + HEAD_DIM] = v[:, _head_cols(kvh)].astype(BF16)
        vb_ref[:, kvh * LANES + HEAD_DIM:(kvh + 1) * LANES] = jnp.zeros((v.shape[0], LANES - HEAD_DIM), BF16)


def _qkv_sample_kernel(h_ref, g_ref, w_ref, q_ref, k_ref, v_ref):
    k, v = _qkv(h_ref, g_ref, w_ref, q_ref)
    for kvh in range(N_KV_HEADS):
        k_ref[:, kvh, :] = k[:, _head_cols(kvh)]
        v_ref[:, kvh, :] = v[:, _head_cols(kvh)]


def _qkv_in_specs(tm, row):
    return [pl.BlockSpec((tm, D_MODEL), row),
            pl.BlockSpec((1, D_MODEL), lambda *_: (0, 0)),
            pl.BlockSpec((D_MODEL, QKV_DIM), lambda *_: (0, 0))]


def qkv_prompt(h, g, w, batch, seq, tm):
    ns = seq // tm
    row = lambda b, i: (b * ns + i, 0)
    tblk = lambda b, i: (b, 0, 0, i)
    return pl.pallas_call(
        _qkv_prompt_kernel,
        grid=(batch, ns),
        in_specs=_qkv_in_specs(tm, row),
        out_specs=[pl.BlockSpec((tm, Q_DIM), row),
                   pl.BlockSpec((None, N_KV_HEADS, HEAD_DIM, tm), tblk),
                   pl.BlockSpec((None, N_KV_HEADS, HEAD_DIM, tm), tblk),
                   pl.BlockSpec((None, N_KV_HEADS, LANES, tm), tblk),
                   pl.BlockSpec((tm, N_KV_HEADS * LANES), row)],
        out_shape=[jax.ShapeDtypeStruct((batch * seq, Q_DIM), BF16),
                   jax.ShapeDtypeStruct((batch, N_KV_HEADS, HEAD_DIM, seq), F32),
                   jax.ShapeDtypeStruct((batch, N_KV_HEADS, HEAD_DIM, seq), F32),
                   jax.ShapeDtypeStruct((batch, N_KV_HEADS, LANES, seq), BF16),
                   jax.ShapeDtypeStruct((batch * seq, N_KV_HEADS * LANES), BF16)],
        compiler_params=_cparams(("parallel", "parallel")),
        name="qkv_prompt",
    )(h, g, w)


def qkv_sample(h, g, w, first_row, n_rows, tm):
    first = first_row // tm
    row_in = lambda i: (first + i, 0)
    row = lambda i: (i, 0)
    row3 = lambda i: (i, 0, 0)
    return pl.pallas_call(
        _qkv_sample_kernel,
        grid=(n_rows // tm,),
        in_specs=_qkv_in_specs(tm, row_in),
        out_specs=[pl.BlockSpec((tm, Q_DIM), row),
                   pl.BlockSpec((tm, N_KV_HEADS, HEAD_DIM), row3),
                   pl.BlockSpec((tm, N_KV_HEADS, HEAD_DIM), row3)],
        out_shape=[jax.ShapeDtypeStruct((n_rows, Q_DIM), BF16),
                   jax.ShapeDtypeStruct((n_rows, N_KV_HEADS, HEAD_DIM), F32),
                   jax.ShapeDtypeStruct((n_rows, N_KV_HEADS, HEAD_DIM), F32)],
        compiler_params=_cparams(("parallel",)),
        name="qkv_sample",
    )(h, g, w)


def _route(xn, wh_ref, wl_ref):
    xh = xn.astype(BF16)
    xl = (xn - xh.astype(F32)).astype(BF16)
    wh = wh_ref[...]
    logits = _dot(xh, wh) + _dot(xl, wh) + _dot(xh, wl_ref[...])
    lane = lax.broadcasted_iota(jnp.int32, logits.shape, 1)
    logits = jnp.where(lane < N_EXPERTS, logits, NEG_INF)
    m1 = jnp.max(logits, axis=1, keepdims=True)
    i1 = jnp.min(jnp.where(logits == m1, lane, ROUTER_LANES), axis=1, keepdims=True)
    rest = jnp.where(lane == i1, NEG_INF, logits)
    m2 = jnp.max(rest, axis=1, keepdims=True)
    i2 = jnp.min(jnp.where(rest == m2, lane, ROUTER_LANES), axis=1, keepdims=True)
    e2 = jnp.exp(m2 - m1)
    g1 = 1.0 / (1.0 + e2)
    out = jnp.where(lane == LANE_EXPERT, i1.astype(F32), 0.0)
    out = jnp.where(lane == LANE_EXPERT + 1, i2.astype(F32), out)
    out = jnp.where(lane == LANE_GATE, g1, out)
    return jnp.where(lane == LANE_GATE + 1, e2 * g1, out)


def _post_attn_kernel(h_ref, ap_ref, as_ref, wo_ref, g_ref, *rest, routed, prompt_tiles):
    if routed:
        wh_ref, wl_ref, h1_ref, xn_ref, route_ref = rest
    else:
        h1_ref, xn_ref = rest
    attn = jnp.where(pl.program_id(0) < prompt_tiles, ap_ref[...], as_ref[...])
    h1 = h_ref[...] + _dot(attn, wo_ref[...])
    h1_ref[...] = h1
    xn = _rms(h1, g_ref[...])
    xn_ref[...] = xn.astype(xn_ref.dtype)
    if routed:
        route_ref[...] = _route(xn, wh_ref, wl_ref)


def post_attn(h, attn_p, attn_s, wo, g, router, tm):
    n = h.shape[0]
    prompt_tiles = attn_p.shape[0] // tm
    sample_tiles = attn_s.shape[0] // tm
    row = lambda i: (i, 0)
    fixed = lambda i: (0, 0)
    routed = router is not None
    in_specs = [pl.BlockSpec((tm, D_MODEL), row),
                pl.BlockSpec((tm, Q_DIM), lambda i: (jnp.minimum(i, prompt_tiles - 1), 0)),
                pl.BlockSpec((tm, Q_DIM), lambda i: (jnp.clip(i - prompt_tiles, 0, sample_tiles - 1), 0)),
                pl.BlockSpec((Q_DIM, D_MODEL), fixed),
                pl.BlockSpec((1, D_MODEL), fixed)]
    out_specs = [pl.BlockSpec((tm, D_MODEL), row), pl.BlockSpec((tm, D_MODEL), row)]
    out_shape = [jax.ShapeDtypeStruct((n, D_MODEL), F32),
                 jax.ShapeDtypeStruct((n, D_MODEL), F32 if routed else BF16)]
    args = [h, attn_p, attn_s, wo, g]
    if routed:
        in_specs += [pl.BlockSpec((D_MODEL, ROUTER_LANES), fixed)] * 2
        out_specs.append(pl.BlockSpec((tm, ROUTER_LANES), row))
        out_shape.append(jax.ShapeDtypeStruct((n, ROUTER_LANES), F32))
        args += list(router)
    return pl.pallas_call(
        functools.partial(_post_attn_kernel, routed=routed, prompt_tiles=prompt_tiles),
        grid=(n // tm,),
        in_specs=in_specs, out_specs=out_specs, out_shape=out_shape,
        compiler_params=_cparams(("parallel",)),
        name="post_attn_routed" if routed else "post_attn",
    )(*args)


def _swiglu_partial(x, wg, wu, wd):
    g = _dot(x, wg)
    u = _dot(x, wu)
    return _dot((g * jax.nn.sigmoid(g) * u).astype(BF16), wd)


def _ffn_kernel(h_ref, x_ref, wg_ref, wu_ref, wd_ref, o_ref, acc_ref):
    f = pl.program_id(1)

    @pl.when(f == 0)
    def _():
        acc_ref[...] = jnp.zeros_like(acc_ref)

    acc_ref[...] += _swiglu_partial(x_ref[...], wg_ref[...], wu_ref[...], wd_ref[...])

    @pl.when(f == pl.num_programs(1) - 1)
    def _():
        o_ref[...] = h_ref[...] + acc_ref[...]


def dense_ffn(h, xn, w_gu, w_down, tm, tf):
    n = h.shape[0]
    d_ff = w_down.shape[0]
    nf = d_ff // tf
    row = lambda i, f: (i, 0)
    mode = dict(pipeline_mode=pl.Buffered(1)) if nf == 1 else {}
    return pl.pallas_call(
        _ffn_kernel,
        grid=(n // tm, nf),
        in_specs=[pl.BlockSpec((tm, D_MODEL), row),
                  pl.BlockSpec((tm, D_MODEL), row),
                  pl.BlockSpec((D_MODEL, tf), lambda i, f: (0, f), **mode),
                  pl.BlockSpec((D_MODEL, tf), lambda i, f: (0, nf + f), **mode),
                  pl.BlockSpec((tf, D_MODEL), lambda i, f: (f, 0), **mode)],
        out_specs=pl.BlockSpec((tm, D_MODEL), row),
        out_shape=jax.ShapeDtypeStruct((n, D_MODEL), F32),
        scratch_shapes=[pltpu.VMEM((tm, D_MODEL), F32)],
        compiler_params=_cparams(("parallel", "arbitrary")),
        name="dense_ffn",
    )(h, xn, w_gu, w_gu, w_down)


def _ple_kernel(h_ref, p_ref, g_ref, wp_ref, wg_ref, *rest, final):
    if final:
        gf_ref, o_ref = rest
    else:
        (o_ref,) = rest
    h = h_ref[...]
    gate = jax.nn.sigmoid(_dot(_rms(h, g_ref[...]).astype(BF16), wg_ref[...]))
    out = h + _dot(p_ref[...].astype(BF16), wp_ref[...]) * gate
    if final:
        out = _rms(out, gf_ref[...])
    o_ref[...] = out


def ple_add(h, p, g, w_proj, w_gate, final_g, tm, first_row=0, n_rows=None):
    n = h.shape[0] if n_rows is None else n_rows
    first = first_row // tm
    ple_dim = p.shape[1]
    row_in = lambda i: (first + i, 0)
    row = lambda i: (i, 0)
    fixed = lambda i: (0, 0)
    final = final_g is not None
    in_specs = [pl.BlockSpec((tm, D_MODEL), row_in),
                pl.BlockSpec((tm, ple_dim), row_in),
                pl.BlockSpec((1, D_MODEL), fixed),
                pl.BlockSpec((ple_dim, D_MODEL), fixed),
                pl.BlockSpec((D_MODEL, D_MODEL), fixed)]
    args = [h, p, g, w_proj, w_gate]
    if final:
        in_specs.append(pl.BlockSpec((1, D_MODEL), fixed))
        args.append(final_g)
    return pl.pallas_call(
        functools.partial(_ple_kernel, final=final),
        grid=(n // tm,),
        in_specs=in_specs,
        out_specs=pl.BlockSpec((tm, D_MODEL), row),
        out_shape=jax.ShapeDtypeStruct((n, D_MODEL), F32),
        compiler_params=_cparams(("parallel",)),
        name="ple_final" if final else "ple",
    )(*args)


def _row_copy(src_ref, src_row, dst_ref, dst_row, sem):
    return pltpu.make_async_copy(src_ref.at[pl.ds(src_row, 1)], dst_ref.at[pl.ds(dst_row, 1)], sem)


def _for_rows(tm, fn):
    def body(g, c):
        base = pl.multiple_of(g * SUBLANES, SUBLANES)
        for k in range(SUBLANES):
            fn(base + k)
        return c

    lax.fori_loop(0, tm // SUBLANES, body, 0)


def _moe_dispatch_kernel(pos_ref, x_ref, xs_in_ref, xs_ref, sem, *, tm):
    del xs_in_ref

    def copies(r):
        return [_row_copy(x_ref, r, xs_ref, pos_ref[s, r], sem) for s in range(TOP_K)]

    def start(r):
        for cp in copies(r):
            cp.start()

    def wait(r):
        for cp in copies(r):
            cp.wait()

    _for_rows(tm, start)
    _for_rows(tm, wait)


def moe_dispatch(xn, pos_tiles, n_rows, tm):
    n = xn.shape[0]
    xs0 = jnp.zeros((n_rows, D_MODEL), F32)
    return pl.pallas_call(
        functools.partial(_moe_dispatch_kernel, tm=tm),
        grid=(n // tm,),
        in_specs=[pl.BlockSpec((None, TOP_K, tm), lambda i: (i, 0, 0), memory_space=pltpu.SMEM),
                  pl.BlockSpec((tm, D_MODEL), lambda i: (i, 0)),
                  pl.BlockSpec(memory_space=pl.ANY)],
        out_specs=pl.BlockSpec(memory_space=pl.ANY),
        out_shape=jax.ShapeDtypeStruct((n_rows, D_MODEL), F32),
        scratch_shapes=[pltpu.SemaphoreType.DMA(())],
        input_output_aliases={2: 0},
        compiler_params=_cparams(("arbitrary",)),
        name="moe_dispatch",
    )(pos_tiles, xn, xs0)


def _moe_grouped_kernel(te_ref, nt_ref, x_ref, wg_ref, wu_ref, wd_ref, o_ref, acc_ref):
    i = pl.program_id(0)
    f = pl.program_id(1)
    live = i < nt_ref[0]

    @pl.when(live & (f == 0))
    def _():
        acc_ref[...] = jnp.zeros_like(acc_ref)

    @pl.when(live)
    def _():
        acc_ref[...] += _swiglu_partial(x_ref[...].astype(BF16), wg_ref[0], wu_ref[0], wd_ref[0])

    @pl.when(live & (f == pl.num_programs(1) - 1))
    def _():
        o_ref[...] = acc_ref[...]

    @pl.when(jnp.logical_not(live) & (f == pl.num_programs(1) - 1))
    def _():
        o_ref[...] = jnp.zeros_like(o_ref)


def moe_grouped(xs, tile_expert, n_tiles, w_gu, w_down, tm, tf):
    n_rows = xs.shape[0]
    max_tiles = n_rows // tm
    d_ff = w_down.shape[1]
    nf = d_ff // tf
    last = lambda i, nt: jnp.minimum(i, nt[0] - 1)
    fblk = lambda i, f, nt: jnp.where(i < nt[0], f, nf - 1)
    row = lambda i, f, te, nt: (last(i, nt), 0)
    grid_spec = pltpu.PrefetchScalarGridSpec(
        num_scalar_prefetch=2,
        grid=(max_tiles, nf),
        in_specs=[pl.BlockSpec((tm, D_MODEL), row),
                  pl.BlockSpec((1, D_MODEL, tf), lambda i, f, te, nt: (te[last(i, nt)], 0, fblk(i, f, nt))),
                  pl.BlockSpec((1, D_MODEL, tf), lambda i, f, te, nt: (te[last(i, nt)], 0, nf + fblk(i, f, nt))),
                  pl.BlockSpec((1, tf, D_MODEL), lambda i, f, te, nt: (te[last(i, nt)], fblk(i, f, nt), 0))],
        out_specs=pl.BlockSpec((tm, D_MODEL), lambda i, f, te, nt: (i, 0)),
        scratch_shapes=[pltpu.VMEM((tm, D_MODEL), F32)])
    return pl.pallas_call(
        _moe_grouped_kernel,
        grid_spec=grid_spec,
        out_shape=jax.ShapeDtypeStruct((n_rows, D_MODEL), F32),
        compiler_params=_cparams(("arbitrary", "arbitrary")),
        name="moe_grouped",
    )(tile_expert, n_tiles, xs, w_gu, w_gu, w_down)


def _moe_combine_kernel(pos_ref, h_ref, route_ref, ys_ref, o_ref, ybuf_ref, sem, *, tm):
    def copies(r):
        return [_row_copy(ys_ref, pos_ref[s, r], ybuf_ref.at[s], r, sem) for s in range(TOP_K)]

    def start(r):
        for cp in copies(r):
            cp.start()

    def wait(r):
        for cp in copies(r):
            cp.wait()

    _for_rows(tm, start)
    route = route_ref[...]
    lane = lax.broadcasted_iota(jnp.int32, route.shape, 1)
    gates = [jnp.sum(jnp.where(lane == LANE_GATE + s, route, 0.0), axis=1, keepdims=True)
             for s in range(TOP_K)]
    _for_rows(tm, wait)
    out = gates[0] * ybuf_ref[0]
    for s in range(1, TOP_K):
        out = out + gates[s] * ybuf_ref[s]
    o_ref[...] = h_ref[...] + out


def moe_combine(h, route, ys, pos_tiles, tm):
    n = h.shape[0]
    row = lambda i: (i, 0)
    return pl.pallas_call(
        functools.partial(_moe_combine_kernel, tm=tm),
        grid=(n // tm,),
        in_specs=[pl.BlockSpec((None, TOP_K, tm), lambda i: (i, 0, 0), memory_space=pltpu.SMEM),
                  pl.BlockSpec((tm, D_MODEL), row),
                  pl.BlockSpec((tm, ROUTER_LANES), row),
                  pl.BlockSpec(memory_space=pl.ANY)],
        out_specs=pl.BlockSpec((tm, D_MODEL), row),
        out_shape=jax.ShapeDtypeStruct((n, D_MODEL), F32),
        scratch_shapes=[pltpu.VMEM((TOP_K, tm, D_MODEL), F32), pltpu.SemaphoreType.DMA(())],
        compiler_params=_cparams(("arbitrary",)),
        name="moe_combine",
    )(pos_tiles, h, route, ys)


def _routing_tables(route, tm):
    n = route.shape[0]
    max_tiles = -(-(TOP_K * n + N_EXPERTS * (tm - 1)) // tm)
    expert = jnp.concatenate([route[:, LANE_EXPERT + s] for s in range(TOP_K)]).astype(jnp.int32)
    onehot = (expert[:, None] == jnp.arange(N_EXPERTS, dtype=jnp.int32)[None, :]).astype(jnp.int32)
    csum = jnp.cumsum(onehot, axis=0)
    counts = csum[-1]
    tiles = (counts + tm - 1) // tm
    tile_end = jnp.cumsum(tiles)
    row_start = (tile_end - tiles) * tm
    pos = jnp.sum(onehot * (csum - 1 + row_start[None, :]), axis=1)
    pos_tiles = pos.reshape(TOP_K, n // tm, tm).transpose(1, 0, 2)
    tile_expert = jnp.sum(jnp.arange(max_tiles, dtype=jnp.int32)[:, None] >= tile_end[None, :], axis=1)
    tile_expert = jnp.minimum(tile_expert, N_EXPERTS - 1).astype(jnp.int32)
    return pos_tiles, tile_expert, tile_end[-1:].astype(jnp.int32), max_tiles * tm


def moe_ffn(h, xn, route, w_gu, w_down, tm, tf):
    pos_tiles, tile_expert, n_tiles, n_rows = _routing_tables(route, tm)
    xs = moe_dispatch(xn, pos_tiles, n_rows, tm)
    ys = moe_grouped(xs, tile_expert, n_tiles, w_gu, w_down, tm, tf)
    return moe_combine(h, route, ys, pos_tiles, tm)


def _later_keys_matrix(tk, with_row_sum):
    j = np.arange(tk)[:, None]
    s = np.arange(2 * tk if with_row_sum else tk)[None, :]
    return jnp.asarray(np.where(s < tk, j > s, True), dtype=BF16)


def _split_bias(bias):
    parts = []
    rest = bias.astype(F32)
    for _ in range(BIAS_PARTS):
        p = rest.astype(BF16).astype(F32)
        parts.append(p)
        rest = rest - p
    return jnp.stack(parts)


def _sb_prompt_kernel(bias_ref, q_ref, k_ref, v_ref, m_ref, o_ref, q4_ref, acc_ref, car_ref, *, tq, tk):
    qi = pl.program_id(1)
    rows = GROUP * tq
    row_g = lax.broadcasted_iota(jnp.int32, (rows, HEAD_DIM), 0) // tq
    lane = lax.broadcasted_iota(jnp.int32, (rows, HEAD_DIM), 1)
    for kvh in range(N_KV_HEADS):
        q4 = jnp.concatenate([q_ref[:, _head_cols(kvh * GROUP + g)] for g in range(GROUP)], axis=0)
        bcols = jnp.zeros((rows, HEAD_DIM), F32)
        for g in range(GROUP):
            for part in range(BIAS_PARTS):
                bcols = jnp.where((row_g == g) & (lane == part), bias_ref[part, kvh * GROUP + g], bcols)
        q4_ref[kvh] = jnp.concatenate([q4, bcols.astype(BF16)], axis=1)
    acc_ref[...] = jnp.zeros_like(acc_ref)
    car_ref[...] = jnp.zeros_like(car_ref)

    def block(j, mask):
        start = pl.multiple_of(j * tk, tk)
        later = m_ref[...]
        for kvh in range(N_KV_HEADS):
            k_t = k_ref[kvh, :, pl.ds(start, tk)]
            v = v_ref[pl.ds(start, tk), kvh * LANES:(kvh + 1) * LANES]
            for r0 in range(0, rows, SB_ROW_CHUNK):
                rs = slice(r0, r0 + SB_ROW_CHUNK)
                z = _dot(q4_ref[kvh, rs], k_t)
                sp = _softplus(z)
                ls = z - sp
                if mask is not None:
                    sp = jnp.where(mask[rs], sp, 0.0)
                sp16 = sp.astype(BF16)
                tail = _dot(sp16, later)
                a = jnp.exp(ls - tail - car_ref[kvh, rs])
                if mask is not None:
                    a = jnp.where(mask[rs], a, 0.0)
                acc_ref[kvh, rs] += _dot(a.astype(BF16), v)
                car_ref[kvh, rs] += tail[:, :1] + sp16[:, :1].astype(F32)

    jd = (qi * tq) // tk
    t_pos = qi * tq + lax.broadcasted_iota(jnp.int32, (rows, tk), 0) % tq
    s_pos = jd * tk + lax.broadcasted_iota(jnp.int32, (rows, tk), 1)
    block(jd, s_pos < t_pos)

    def body(i, c):
        block(jd - 1 - i, None)
        return c

    lax.fori_loop(0, jd, body, 0)

    for kvh in range(N_KV_HEADS):
        for g in range(GROUP):
            o_ref[:, _head_cols(kvh * GROUP + g)] = acc_ref[kvh, g * tq:(g + 1) * tq, :HEAD_DIM].astype(BF16)


def sb_prompt(q, kb, vb, bias, batch, seq, tq, tk):
    nq = seq // tq
    rows = GROUP * tq
    return pl.pallas_call(
        functools.partial(_sb_prompt_kernel, tq=tq, tk=tk),
        grid=(batch, nq),
        in_specs=[pl.BlockSpec(memory_space=pltpu.SMEM),
                  pl.BlockSpec((tq, Q_DIM), lambda b, i: (b * nq + i, 0)),
                  pl.BlockSpec((None, N_KV_HEADS, LANES, seq), lambda b, i: (b, 0, 0, 0)),
                  pl.BlockSpec((seq, N_KV_HEADS * LANES), lambda b, i: (b, 0)),
                  pl.BlockSpec((tk, tk), lambda b, i: (0, 0))],
        out_specs=pl.BlockSpec((tq, Q_DIM), lambda b, i: (b * nq + i, 0)),
        out_shape=jax.ShapeDtypeStruct((batch * seq, Q_DIM), BF16),
        scratch_shapes=[pltpu.VMEM((N_KV_HEADS, rows, LANES), BF16),
                        pltpu.VMEM((N_KV_HEADS, rows, LANES), F32),
                        pltpu.VMEM((N_KV_HEADS, rows, 1), F32)],
        compiler_params=_cparams(("parallel", "parallel")),
        name="sb_prompt",
    )(_split_bias(bias), q, kb, vb, _later_keys_matrix(tk, False))


def _sb_sample_kernel(pt_ref, bias_ref, q_ref, kn_ref, vn_ref, m_ref, pool_k_ref, pool_v_ref, o_ref,
                      q4_ref, bt_ref, acc_ref, car_ref, kbuf_ref, vbuf_ref, sems, *, t, npg, n_pages, layer):
    b = pl.program_id(0)
    c = pl.program_id(1)
    nc = pl.num_programs(1)
    rows = GROUP * t
    hrows = N_KV_HEADS * rows
    later = m_ref[...]
    heads = range(N_KV_HEADS)

    step = b * nc + c
    slot = step % 2

    def page_copies(seq, chunk, slot_):
        cps = []
        for i in range(npg):
            page = pt_ref[seq * n_pages + (n_pages - 1 - (chunk * npg + i))]
            cps.append(pltpu.make_async_copy(pool_k_ref.at[layer, page], kbuf_ref.at[slot_, i], sems.at[0, slot_]))
            cps.append(pltpu.make_async_copy(pool_v_ref.at[layer, page], vbuf_ref.at[slot_, i], sems.at[1, slot_]))
        return cps

    def start_all(cps):
        for j, cp in enumerate(cps):
            cp.start(priority=j // 2 % 2)

    @pl.when(step == 0)
    def _():
        start_all(page_copies(b, c, slot))

    @pl.when(step + 1 < pl.num_programs(0) * nc)
    def _():
        last_chunk = c == nc - 1
        start_all(page_copies(jnp.where(last_chunk, b + 1, b), jnp.where(last_chunk, 0, c + 1), 1 - slot))

    for cp in page_copies(b, c, slot):
        cp.wait()
    k_pages = [kbuf_ref.at[slot, i] for i in range(npg)]
    v_pages = [vbuf_ref.at[slot, i] for i in range(npg)]

    @pl.when(c == 0)
    def _():
        q = q_ref[...].astype(F32)
        row_h = lax.broadcasted_iota(jnp.int32, (hrows, PAGE_SIZE), 0) // t
        bt = jnp.zeros((hrows, PAGE_SIZE), F32)
        for h in range(N_HEADS):
            bt = jnp.where(row_h == h, bias_ref[h], bt)
        bt_ref[...] = bt
        for kvh in heads:
            q4_ref[kvh] = jnp.concatenate(
                [q[:, _head_cols(kvh * GROUP + g)] for g in range(GROUP)], axis=0).astype(BF16)
        acc_ref[...] = jnp.zeros_like(acc_ref)
        car_ref[...] = jnp.zeros_like(car_ref)

    t_in = lax.broadcasted_iota(jnp.int32, (hrows, PAGE_SIZE), 0) % t
    s_in = lax.broadcasted_iota(jnp.int32, (hrows, PAGE_SIZE), 1)
    own = (s_in < t_in) & (c == 0)
    pad = jnp.zeros((PAGE_SIZE - t, HEAD_DIM), F32)

    def own_rows(ref, kvh):
        return jnp.concatenate([ref[:, kvh, :], pad], axis=0).astype(BF16)

    def pages_cat(refs, kvh):
        return jnp.concatenate([r[kvh].astype(BF16) for r in refs], axis=1)

    def own_masked(x):
        return jnp.concatenate([jnp.where(own, x[:hrows], 0.0), x[hrows:]], axis=0)

    bt = bt_ref[...]
    z_own = jnp.concatenate([_dot_nt(q4_ref[kvh], own_rows(kn_ref, kvh)) for kvh in heads], axis=0) + bt
    zc = [_dot(q4_ref[kvh], pages_cat(k_pages, kvh)) for kvh in heads]
    z = jnp.concatenate(
        [z_own] +
        [jnp.concatenate([zc[kvh][:, i * PAGE_SIZE:(i + 1) * PAGE_SIZE] for kvh in heads], axis=0) + bt
         for i in range(npg)], axis=0)
    sp = own_masked(_softplus(z))
    res = _dot(sp.astype(BF16), later)
    inc = res[:, PAGE_SIZE:]
    carry = car_ref[...]
    carries = []
    for i in range(npg + 1):
        carries.append(carry)
        carry = carry + inc[i * hrows:(i + 1) * hrows]
    car_ref[...] = carry
    a = own_masked(jnp.exp(z - sp - res[:, :PAGE_SIZE] - jnp.concatenate(carries, axis=0))).astype(BF16)
    for kvh in heads:
        lo = kvh * rows
        a_cat = jnp.concatenate(
            [a[(i + 1) * hrows + lo:(i + 1) * hrows + lo + rows] for i in range(npg)], axis=1)
        acc_ref[kvh] += (_dot(a[lo:lo + rows], own_rows(vn_ref, kvh))
                         + _dot_nt(a_cat, pages_cat(v_pages, kvh)))

    @pl.when(c == pl.num_programs(1) - 1)
    def _():
        for kvh in heads:
            for g in range(GROUP):
                o_ref[:, _head_cols(kvh * GROUP + g)] = acc_ref[kvh, g * t:(g + 1) * t, :].astype(BF16)


def sb_sample(q_s, k_new, v_new, pool_k, pool_v, page_table, layer, bias):
    db, t, _ = q_s.shape
    n_pages = page_table.shape[1]
    npg = PAGES_PER_STEP
    assert n_pages % npg == 0, (n_pages, npg)
    nc = n_pages // npg
    rows = GROUP * t
    seq_blk = lambda b, c, pt: (b, 0, 0)
    seq_blk4 = lambda b, c, pt: (b, 0, 0, 0)

    page_buf = pltpu.VMEM((2, npg, N_KV_HEADS, HEAD_DIM, PAGE_SIZE), F32)
    grid_spec = pltpu.PrefetchScalarGridSpec(
        num_scalar_prefetch=1,
        grid=(db, nc),
        in_specs=[pl.BlockSpec(memory_space=pltpu.SMEM),
                  pl.BlockSpec((None, t, Q_DIM), seq_blk),
                  pl.BlockSpec((None, t, N_KV_HEADS, HEAD_DIM), seq_blk4),
                  pl.BlockSpec((None, t, N_KV_HEADS, HEAD_DIM), seq_blk4),
                  pl.BlockSpec((PAGE_SIZE, 2 * PAGE_SIZE), lambda b, c, pt: (0, 0)),
                  pl.BlockSpec(memory_space=pl.ANY),
                  pl.BlockSpec(memory_space=pl.ANY)],
        out_specs=pl.BlockSpec((None, t, Q_DIM), seq_blk),
        scratch_shapes=[pltpu.VMEM((N_KV_HEADS, rows, HEAD_DIM), BF16),
                        pltpu.VMEM((N_KV_HEADS * rows, PAGE_SIZE), F32),
                        pltpu.VMEM((N_KV_HEADS, rows, HEAD_DIM), F32),
                        pltpu.VMEM((N_KV_HEADS * rows, PAGE_SIZE), F32),
                        page_buf, page_buf,
                        pltpu.SemaphoreType.DMA((2, 2))])
    return pl.pallas_call(
        functools.partial(_sb_sample_kernel, t=t, npg=npg, n_pages=n_pages, layer=layer),
        grid_spec=grid_spec,
        out_shape=jax.ShapeDtypeStruct((db, t, Q_DIM), BF16),
        compiler_params=_cparams(("arbitrary", "arbitrary")),
        name="sb_sample",
    )(page_table.reshape(-1), bias, q_s, k_new, v_new, _later_keys_matrix(PAGE_SIZE, True), pool_k, pool_v)


def _alibi_slopes():
    h = np.arange(1, N_HEADS + 1, dtype=np.float64)
    return np.exp2(-8.0 * h / N_HEADS).reshape(N_KV_HEADS, GROUP)


def _swa_tile(t_q, dist):
    valid = (dist >= 0) & (dist < WINDOW)
    tile = np.where(valid[None, None], -_alibi_slopes()[:, :, None, None] * dist[None, None], -np.inf)
    return jnp.asarray(tile.reshape(N_KV_HEADS, GROUP * t_q, dist.shape[1]), dtype=F32)


def _swa_probs(s, sink):
    m = jnp.maximum(jnp.max(s, axis=1, keepdims=True), sink)
    p = jnp.exp(s - m)
    denom = jnp.sum(p, axis=1, keepdims=True) + jnp.exp(sink - m)
    return (p * (1.0 / denom)).astype(BF16)


def _sink_col(sinks_ref, kvh, rows, t):
    row_g = lax.broadcasted_iota(jnp.int32, (rows, 1), 0) // t
    col = jnp.zeros((rows, 1), F32)
    for g in range(GROUP):
        col = jnp.where(row_g == g, sinks_ref[kvh * GROUP + g], col)
    return col


def _swa_prompt_kernel(sinks_ref, q_ref, kp_ref, kc_ref, vp_ref, vc_ref, tile_ref, o_ref, *, tq):
    qi = pl.program_id(1)
    rows = GROUP * tq
    col = lax.broadcasted_iota(jnp.int32, (rows, 2 * tq), 1)
    keep = (col >= tq) | (qi > 0)
    vband = jnp.concatenate([vp_ref[...], vc_ref[...]], axis=0)
    for kvh in range(N_KV_HEADS):
        q4 = jnp.concatenate([q_ref[:, _head_cols(kvh * GROUP + g)] for g in range(GROUP)], axis=0)
        kband_t = jnp.concatenate([kp_ref[kvh, :HEAD_DIM], kc_ref[kvh, :HEAD_DIM]], axis=1)
        s = _dot(q4, kband_t) + jnp.where(keep, tile_ref[kvh], NEG_INF)
        o = _dot(_swa_probs(s, _sink_col(sinks_ref, kvh, rows, tq)),
                 vband[:, kvh * LANES:kvh * LANES + HEAD_DIM])
        for g in range(GROUP):
            o_ref[:, _head_cols(kvh * GROUP + g)] = o[g * tq:(g + 1) * tq].astype(BF16)


def swa_prompt(q, kb, vb, sinks, batch, seq):
    tq = WINDOW
    nq = seq // tq
    qi = np.arange(tq)[:, None]
    kj = np.arange(2 * tq)[None, :]
    tile = _swa_tile(tq, tq + qi - kj)
    cur = lambda b, i: (b * nq + i, 0)
    prev = lambda b, i: (b * nq + jnp.maximum(i - 1, 0), 0)
    cur_t = lambda b, i: (b, 0, 0, i)
    prev_t = lambda b, i: (b, 0, 0, jnp.maximum(i - 1, 0))
    return pl.pallas_call(
        functools.partial(_swa_prompt_kernel, tq=tq),
        grid=(batch, nq),
        in_specs=[pl.BlockSpec(memory_space=pltpu.SMEM),
                  pl.BlockSpec((tq, Q_DIM), cur),
                  pl.BlockSpec((None, N_KV_HEADS, LANES, tq), prev_t),
                  pl.BlockSpec((None, N_KV_HEADS, LANES, tq), cur_t),
                  pl.BlockSpec((tq, N_KV_HEADS * LANES), prev),
                  pl.BlockSpec((tq, N_KV_HEADS * LANES), cur),
                  pl.BlockSpec((N_KV_HEADS, GROUP * tq, 2 * tq), lambda b, i: (0, 0, 0))],
        out_specs=pl.BlockSpec((tq, Q_DIM), cur),
        out_shape=jax.ShapeDtypeStruct((batch * seq, Q_DIM), BF16),
        compiler_params=_cparams(("parallel", "parallel")),
        name="swa_prompt",
    )(sinks, q, kb, kb, vb, vb, tile)


def _swa_sample_kernel(sinks_ref, q_ref, kn_ref, vn_ref, kbuf_ref, vbuf_ref, tile_ref,
                       o_ref, ko_ref, vo_ref, *, t, w_buf):
    rows = GROUP * t
    row_pad = jnp.zeros((w_buf - t, HEAD_DIM), F32)
    lane_pad = jnp.zeros((w_buf, LANES - HEAD_DIM), F32)
    sink_cols = [_sink_col(sinks_ref, kvh, rows, t) for kvh in range(N_KV_HEADS)]

    def shifted(buf_t, new):
        new_t = jnp.concatenate([new, lane_pad], axis=1).T[:HEAD_DIM, :t]
        return jnp.concatenate([buf_t[:, t:], new_t], axis=1)

    for s in range(q_ref.shape[0]):
        q = q_ref[s].astype(F32)
        for kvh in range(N_KV_HEADS):
            kbuf_t = kbuf_ref[s, kvh]
            vbuf_t = vbuf_ref[s, kvh]
            kn = jnp.concatenate([kn_ref[s, :, kvh, :], row_pad], axis=0)
            vn = jnp.concatenate([vn_ref[s, :, kvh, :], row_pad], axis=0)
            ko_ref[s, kvh] = shifted(kbuf_t, kn)
            vo_ref[s, kvh] = shifted(vbuf_t, vn)
            q4 = jnp.concatenate(
                [q[:, _head_cols(kvh * GROUP + g)] for g in range(GROUP)], axis=0).astype(BF16)
            sc = jnp.concatenate([_dot(q4, kbuf_t.astype(BF16)), _dot_nt(q4, kn.astype(BF16))], axis=1)
            p = _swa_probs(sc + tile_ref[kvh], sink_cols[kvh])
            o = _dot_nt(p[:, :w_buf], vbuf_t.astype(BF16)) + _dot(p[:, w_buf:], vn.astype(BF16))
            for g in range(GROUP):
                o_ref[s, :, _head_cols(kvh * GROUP + g)] = o[g * t:(g + 1) * t].astype(BF16)


def swa_sample(q_s, k_new, v_new, buf_k, buf_v, layer, sinks):
    db, t, _ = q_s.shape
    w_buf = buf_k.shape[-1]
    ns = SWA_SEQS_PER_STEP
    idx = np.arange(2 * w_buf)[None, :]
    dist = np.where(idx < w_buf + t, (w_buf + np.arange(t))[:, None] - idx, -1)
    tile = _swa_tile(t, dist)
    seq_blk = lambda i: (i, 0, 0)
    seq_blk4 = lambda i: (i, 0, 0, 0)
    buf_blk = lambda i: (layer, i, 0, 0, 0)
    kv_shape = (ns, N_KV_HEADS, HEAD_DIM, w_buf)
    return pl.pallas_call(
        functools.partial(_swa_sample_kernel, t=t, w_buf=w_buf),
        grid=(db // ns,),
        in_specs=[pl.BlockSpec(memory_space=pltpu.SMEM),
                  pl.BlockSpec((ns, t, Q_DIM), seq_blk),
                  pl.BlockSpec((ns, t, N_KV_HEADS, HEAD_DIM), seq_blk4),
                  pl.BlockSpec((ns, t, N_KV_HEADS, HEAD_DIM), seq_blk4),
                  pl.BlockSpec((None,) + kv_shape, buf_blk),
                  pl.BlockSpec((None,) + kv_shape, buf_blk),
                  pl.BlockSpec((N_KV_HEADS, GROUP * t, 2 * w_buf), lambda i: (0, 0, 0))],
        out_specs=[pl.BlockSpec((ns, t, Q_DIM), seq_blk),
                   pl.BlockSpec(kv_shape, seq_blk4),
                   pl.BlockSpec(kv_shape, seq_blk4)],
        out_shape=[jax.ShapeDtypeStruct((db, t, Q_DIM), BF16),
                   jax.ShapeDtypeStruct((db, N_KV_HEADS, HEAD_DIM, w_buf), F32),
                   jax.ShapeDtypeStruct((db, N_KV_HEADS, HEAD_DIM, w_buf), F32)],
        compiler_params=_cparams(("parallel",)),
        name="swa_sample",
    )(sinks, q_s, k_new, v_new, buf_k, buf_v, tile)


def _pad_router(w):
    w = jnp.pad(w, ((0, 0), (0, ROUTER_LANES - w.shape[1])))
    hi = w.astype(BF16)
    return hi, (w - hi.astype(F32)).astype(BF16)


def _token_tile(n_p, n_s):
    for tm in (512, 256, 128, 64, 32, 16, 8):
        if n_p % tm == 0 and n_s % tm == 0:
            return tm
    raise ValueError(f"token counts {n_p}, {n_s} are not multiples of 8")


def kernel(x_prompt, x_sample, cache_sb_k, cache_sb_v, cache_swa_k, cache_swa_v, page_table,
           p_prompt, p_sample, attn_norm, w_qkv, w_o, sb_bias, attn_sinks, ffn_norm, w_gu_dense,
           w_down_dense, w_router, w_gu_expert, w_down_expert, ple_norm, w_ple_proj,
           w_ple_gate, final_norm):
    batch, seq, _ = x_prompt.shape
    db, t, _ = x_sample.shape
    depth = attn_norm.shape[0]
    n_p = batch * seq
    n_s = db * t
    tm = _token_tile(n_p, n_s)
    assert seq % SB_KEY_BLOCK == 0 and seq % WINDOW == 0

    h = jnp.concatenate([x_prompt.reshape(n_p, D_MODEL), x_sample.reshape(n_s, D_MODEL)], axis=0)
    p_all = jnp.concatenate([p_prompt.reshape(depth, n_p, -1), p_sample.reshape(depth, n_s, -1)], axis=1)
    w_keep = min(WINDOW, seq)
    d_ff = w_down_dense.shape[1]
    d_ffe = w_down_expert.shape[2]
    to_t = lambda x: jnp.moveaxis(x, -3, -1)
    from_t = lambda x: jnp.moveaxis(x, -1, -3)
    pool_k, pool_v = to_t(cache_sb_k), to_t(cache_sb_v)
    buf_k, buf_v = to_t(cache_swa_k), to_t(cache_swa_v)

    sb_kp, sb_vp, sb_ks, sb_vs = [], [], [], []
    sw_kp, sw_vp, sw_ks, sw_vs = [], [], [], []
    for i in range(depth):
        j = i // 2
        g_attn = attn_norm[i][None]
        w_i = w_qkv[i].astype(BF16)
        q_p, kt_p, vt_p, kb, vb = qkv_prompt(h, g_attn, w_i, batch, seq, tm)
        q_s, k_s, v_s = qkv_sample(h, g_attn, w_i, n_p, n_s, tm)
        q_s = q_s.reshape(db, t, Q_DIM)
        k_s = k_s.reshape(db, t, N_KV_HEADS, HEAD_DIM)
        v_s = v_s.reshape(db, t, N_KV_HEADS, HEAD_DIM)
        if i % 2 == 0:
            o_p = sb_prompt(q_p, kb, vb, sb_bias[j], batch, seq, SB_QUERY_BLOCK, SB_KEY_BLOCK)
            o_s = sb_sample(q_s, k_s, v_s, pool_k, pool_v, page_table, j, sb_bias[j])
            sb_kp.append(from_t(kt_p)); sb_vp.append(from_t(vt_p)); sb_ks.append(k_s); sb_vs.append(v_s)
        else:
            o_p = swa_prompt(q_p, kb, vb, attn_sinks[j], batch, seq)
            o_s, bks, bvs = swa_sample(q_s, k_s, v_s, buf_k, buf_v, j, attn_sinks[j])
            sw_kp.append(from_t(kt_p[..., seq - w_keep:])); sw_vp.append(from_t(vt_p[..., seq - w_keep:]))
            sw_ks.append(from_t(bks)); sw_vs.append(from_t(bvs))
        o_s = o_s.reshape(n_s, Q_DIM)
        if i % 2 == 0:
            h1, xn = post_attn(h, o_p, o_s, w_o[i].astype(BF16), ffn_norm[i][None], None, tm)
            h2 = dense_ffn(h1, xn, w_gu_dense[j].astype(BF16), w_down_dense[j].astype(BF16), tm, d_ff)
        else:
            h1, xn, route = post_attn(h, o_p, o_s, w_o[i].astype(BF16), ffn_norm[i][None],
                                      _pad_router(w_router[j]), tm)
            h2 = moe_ffn(h1, xn, route, w_gu_expert[j].astype(BF16), w_down_expert[j].astype(BF16),
                         tm, d_ffe // 2)
        ple_args = (h2, p_all[i], ple_norm[i][None], w_ple_proj[i].astype(BF16), w_ple_gate[i].astype(BF16))
        if i < depth - 1:
            h = ple_add(*ple_args, None, tm)
        else:
            y_p = ple_add(*ple_args, final_norm[None], tm, 0, n_p)
            y_s = ple_add(*ple_args, final_norm[None], tm, n_p, n_s)

    y_prompt = y_p.reshape(batch, seq, D_MODEL)
    y_sample = y_s.reshape(db, t, D_MODEL)
    return (y_prompt, y_sample,
            jnp.stack(sb_kp), jnp.stack(sb_vp), jnp.stack(sb_ks), jnp.stack(sb_vs),
            jnp.stack(sw_kp), jnp.stack(sw_vp), jnp.stack(sw_ks), jnp.stack(sw_vs))
```

```python
import functools

import numpy as np
import jax
import jax.numpy as jnp
from jax import lax
from jax.experimental import pallas as pl
from jax.experimental.pallas import tpu as pltpu

F32 = jnp.float32
BF16 = jnp.bfloat16

D_MODEL = 1024
N_HEADS = 16
HEAD_DIM = 64
N_KV_HEADS = 4
GROUP = N_HEADS // N_KV_HEADS
Q_DIM = N_HEADS * HEAD_DIM
KV_DIM = N_KV_HEADS * HEAD_DIM
QKV_DIM = Q_DIM + 2 * KV_DIM
ATTN_SCALE = HEAD_DIM ** -0.5
WINDOW = 128
PAGE_SIZE = 128
N_EXPERTS = 8
TOP_K = 2
RMS_EPS = 1e-6

LANES = 128
ROUTER_LANES = LANES
VMEM_LIMIT = 56 * 1024 * 1024
PAGES_PER_STEP = 32
SWA_SEQS_PER_STEP = 8
SB_KEY_BLOCK = 256
SB_QUERY_BLOCK = 256
SB_ROW_CHUNK = GROUP * SB_QUERY_BLOCK
SUBLANES = 8
NEG_INF = float("-inf")
BIAS_PARTS = 3
LANE_EXPERT = N_EXPERTS
LANE_GATE = N_EXPERTS + TOP_K


def _cparams(sem, vmem=VMEM_LIMIT):
    return pltpu.CompilerParams(dimension_semantics=sem, vmem_limit_bytes=vmem)


def _dot(a, b):
    return jnp.dot(a, b, preferred_element_type=F32)


def _dot_nt(a, b):
    return lax.dot_general(a, b, (((1,), (1,)), ((), ())), preferred_element_type=F32)


def _head_cols(h):
    return slice(h * HEAD_DIM, (h + 1) * HEAD_DIM)


def _rms(x, g):
    ms = jnp.mean(x * x, axis=-1, keepdims=True)
    return x * lax.rsqrt(ms + RMS_EPS) * g


def _neg_abs(z):
    return lax.bitcast_convert_type(lax.bitcast_convert_type(z, jnp.int32) | jnp.int32(-2 ** 31), F32)


def _softplus(z):
    return jnp.maximum(z, 0.0) + jnp.log(1.0 + jnp.exp(_neg_abs(z)))


def _qkv(h_ref, g_ref, w_ref, q_ref):
    xn = _rms(h_ref[...], g_ref[...]).astype(BF16)
    t = _dot(xn, w_ref[...])
    q_ref[...] = (t[:, :Q_DIM] * ATTN_SCALE).astype(BF16)
    return t[:, Q_DIM:Q_DIM + KV_DIM], t[:, Q_DIM + KV_DIM:]


def _qkv_prompt_kernel(h_ref, g_ref, w_ref, q_ref, kt_ref, vt_ref, kb_ref, vb_ref):
    k, v = _qkv(h_ref, g_ref, w_ref, q_ref)
    kt = k.T
    vt = v.T
    row = lax.broadcasted_iota(jnp.int32, (LANES - HEAD_DIM, k.shape[0]), 0)
    ones = jnp.where(row < BIAS_PARTS, 1.0, 0.0).astype(BF16)
    for kvh in range(N_KV_HEADS):
        kt_ref[kvh] = kt[_head_cols(kvh)]
        vt_ref[kvh] = vt[_head_cols(kvh)]
        kb_ref[kvh, :HEAD_DIM] = kt[_head_cols(kvh)].astype(BF16)
        kb_ref[kvh, HEAD_DIM:] = ones
        vb_ref[:, kvh * LANES:kvh * LANES + HEAD_DIM] = v[:, _head_cols(kvh)].astype(BF16)
        vb_ref[:, kvh * LANES + HEAD_DIM:(kvh + 1) * LANES] = jnp.zeros((v.shape[0], LANES - HEAD_DIM), BF16)


def _qkv_sample_kernel(h_ref, g_ref, w_ref, q_ref, k_ref, v_ref):
    k, v = _qkv(h_ref, g_ref, w_ref, q_ref)
    for kvh in range(N_KV_HEADS):
        k_ref[:, kvh, :] = k[:, _head_cols(kvh)]
        v_ref[:, kvh, :] = v[:, _head_cols(kvh)]


def _qkv_in_specs(tm, row):
    return [pl.BlockSpec((tm, D_MODEL), row),
            pl.BlockSpec((1, D_MODEL), lambda *_: (0, 0)),
            pl.BlockSpec((D_MODEL, QKV_DIM), lambda *_: (0, 0))]


def qkv_prompt(h, g, w, batch, seq, tm):
    ns = seq // tm
    row = lambda b, i: (b * ns + i, 0)
    tblk = lambda b, i: (b, 0, 0, i)
    return pl.pallas_call(
        _qkv_prompt_kernel,
        grid=(batch, ns),
        in_specs=_qkv_in_specs(tm, row),
        out_specs=[pl.BlockSpec((tm, Q_DIM), row),
                   pl.BlockSpec((None, N_KV_HEADS, HEAD_DIM, tm), tblk),
                   pl.BlockSpec((None, N_KV_HEADS, HEAD_DIM, tm), tblk),
                   pl.BlockSpec((None, N_KV_HEADS, LANES, tm), tblk),
                   pl.BlockSpec((tm, N_KV_HEADS * LANES), row)],
        out_shape=[jax.ShapeDtypeStruct((batch * seq, Q_DIM), BF16),
                   jax.ShapeDtypeStruct((batch, N_KV_HEADS, HEAD_DIM, seq), F32),
                   jax.ShapeDtypeStruct((batch, N_KV_HEADS, HEAD_DIM, seq), F32),
                   jax.ShapeDtypeStruct((batch, N_KV_HEADS, LANES, seq), BF16),
                   jax.ShapeDtypeStruct((batch * seq, N_KV_HEADS * LANES), BF16)],
        compiler_params=_cparams(("parallel", "parallel")),
        name="qkv_prompt",
    )(h, g, w)


def qkv_sample(h, g, w, first_row, n_rows, tm):
    first = first_row // tm
    row_in = lambda i: (first + i, 0)
    row = lambda i: (i, 0)
    row3 = lambda i: (i, 0, 0)
    return pl.pallas_call(
        _qkv_sample_kernel,
        grid=(n_rows // tm,),
        in_specs=_qkv_in_specs(tm, row_in),
        out_specs=[pl.BlockSpec((tm, Q_DIM), row),
                   pl.BlockSpec((tm, N_KV_HEADS, HEAD_DIM), row3),
                   pl.BlockSpec((tm, N_KV_HEADS, HEAD_DIM), row3)],
        out_shape=[jax.ShapeDtypeStruct((n_rows, Q_DIM), BF16),
                   jax.ShapeDtypeStruct((n_rows, N_KV_HEADS, HEAD_DIM), F32),
                   jax.ShapeDtypeStruct((n_rows, N_KV_HEADS, HEAD_DIM), F32)],
        compiler_params=_cparams(("parallel",)),
        name="qkv_sample",
    )(h, g, w)


def _route(xn, wh_ref, wl_ref):
    xh = xn.astype(BF16)
    xl = (xn - xh.astype(F32)).astype(BF16)
    wh = wh_ref[...]
    logits = _dot(xh, wh) + _dot(xl, wh) + _dot(xh, wl_ref[...])
    lane = lax.broadcasted_iota(jnp.int32, logits.shape, 1)
    logits = jnp.where(lane < N_EXPERTS, logits, NEG_INF)
    m1 = jnp.max(logits, axis=1, keepdims=True)
    i1 = jnp.min(jnp.where(logits == m1, lane, ROUTER_LANES), axis=1, keepdims=True)
    rest = jnp.where(lane == i1, NEG_INF, logits)
    m2 = jnp.max(rest, axis=1, keepdims=True)
    i2 = jnp.min(jnp.where(rest == m2, lane, ROUTER_LANES), axis=1, keepdims=True)
    e2 = jnp.exp(m2 - m1)
    g1 = 1.0 / (1.0 + e2)
    out = jnp.where(lane == LANE_EXPERT, i1.astype(F32), 0.0)
    out = jnp.where(lane == LANE_EXPERT + 1, i2.astype(F32), out)
    out = jnp.where(lane == LANE_GATE, g1, out)
    return jnp.where(lane == LANE_GATE + 1, e2 * g1, out)


def _post_attn_kernel(h_ref, ap_ref, as_ref, wo_ref, g_ref, *rest, routed, prompt_tiles):
    if routed:
        wh_ref, wl_ref, h1_ref, xn_ref, route_ref = rest
    else:
        h1_ref, xn_ref = rest
    attn = jnp.where(pl.program_id(0) < prompt_tiles, ap_ref[...], as_ref[...])
    h1 = h_ref[...] + _dot(attn, wo_ref[...])
    h1_ref[...] = h1
    xn = _rms(h1, g_ref[...])
    xn_ref[...] = xn.astype(xn_ref.dtype)
    if routed:
        route_ref[...] = _route(xn, wh_ref, wl_ref)


def post_attn(h, attn_p, attn_s, wo, g, router, tm):
    n = h.shape[0]
    prompt_tiles = attn_p.shape[0] // tm
    sample_tiles = attn_s.shape[0] // tm
    row = lambda i: (i, 0)
    fixed = lambda i: (0, 0)
    routed = router is not None
    in_specs = [pl.BlockSpec((tm, D_MODEL), row),
                pl.BlockSpec((tm, Q_DIM), lambda i: (jnp.minimum(i, prompt_tiles - 1), 0)),
                pl.BlockSpec((tm, Q_DIM), lambda i: (jnp.clip(i - prompt_tiles, 0, sample_tiles - 1), 0)),
                pl.BlockSpec((Q_DIM, D_MODEL), fixed),
                pl.BlockSpec((1, D_MODEL), fixed)]
    out_specs = [pl.BlockSpec((tm, D_MODEL), row), pl.BlockSpec((tm, D_MODEL), row)]
    out_shape = [jax.ShapeDtypeStruct((n, D_MODEL), F32),
                 jax.ShapeDtypeStruct((n, D_MODEL), F32 if routed else BF16)]
    args = [h, attn_p, attn_s, wo, g]
    if routed:
        in_specs += [pl.BlockSpec((D_MODEL, ROUTER_LANES), fixed)] * 2
        out_specs.append(pl.BlockSpec((tm, ROUTER_LANES), row))
        out_shape.append(jax.ShapeDtypeStruct((n, ROUTER_LANES), F32))
        args += list(router)
    return pl.pallas_call(
        functools.partial(_post_attn_kernel, routed=routed, prompt_tiles=prompt_tiles),
        grid=(n // tm,),
        in_specs=in_specs, out_specs=out_specs, out_shape=out_shape,
        compiler_params=_cparams(("parallel",)),
        name="post_attn_routed" if routed else "post_attn",
    )(*args)


def _swiglu_partial(x, wg, wu, wd):
    g = _dot(x, wg)
    u = _dot(x, wu)
    return _dot((g * jax.nn.sigmoid(g) * u).astype(BF16), wd)


def _ffn_kernel(h_ref, x_ref, wg_ref, wu_ref, wd_ref, o_ref, acc_ref):
    f = pl.program_id(1)

    @pl.when(f == 0)
    def _():
        acc_ref[...] = jnp.zeros_like(acc_ref)

    acc_ref[...] += _swiglu_partial(x_ref[...], wg_ref[...], wu_ref[...], wd_ref[...])

    @pl.when(f == pl.num_programs(1) - 1)
    def _():
        o_ref[...] = h_ref[...] + acc_ref[...]


def dense_ffn(h, xn, w_gu, w_down, tm, tf):
    n = h.shape[0]
    d_ff = w_down.shape[0]
    nf = d_ff // tf
    row = lambda i, f: (i, 0)
    mode = dict(pipeline_mode=pl.Buffered(1)) if nf == 1 else {}
    return pl.pallas_call(
        _ffn_kernel,
        grid=(n // tm, nf),
        in_specs=[pl.BlockSpec((tm, D_MODEL), row),
                  pl.BlockSpec((tm, D_MODEL), row),
                  pl.BlockSpec((D_MODEL, tf), lambda i, f: (0, f), **mode),
                  pl.BlockSpec((D_MODEL, tf), lambda i, f: (0, nf + f), **mode),
                  pl.BlockSpec((tf, D_MODEL), lambda i, f: (f, 0), **mode)],
        out_specs=pl.BlockSpec((tm, D_MODEL), row),
        out_shape=jax.ShapeDtypeStruct((n, D_MODEL), F32),
        scratch_shapes=[pltpu.VMEM((tm, D_MODEL), F32)],
        compiler_params=_cparams(("parallel", "arbitrary")),
        name="dense_ffn",
    )(h, xn, w_gu, w_gu, w_down)


def _ple_kernel(h_ref, p_ref, g_ref, wp_ref, wg_ref, *rest, final):
    if final:
        gf_ref, o_ref = rest
    else:
        (o_ref,) = rest
    h = h_ref[...]
    gate = jax.nn.sigmoid(_dot(_rms(h, g_ref[...]).astype(BF16), wg_ref[...]))
    out = h + _dot(p_ref[...].astype(BF16), wp_ref[...]) * gate
    if final:
        out = _rms(out, gf_ref[...])
    o_ref[...] = out


def ple_add(h, p, g, w_proj, w_gate, final_g, tm, first_row=0, n_rows=None):
    n = h.shape[0] if n_rows is None else n_rows
    first = first_row // tm
    ple_dim = p.shape[1]
    row_in = lambda i: (first + i, 0)
    row = lambda i: (i, 0)
    fixed = lambda i: (0, 0)
    final = final_g is not None
    in_specs = [pl.BlockSpec((tm, D_MODEL), row_in),
                pl.BlockSpec((tm, ple_dim), row_in),
                pl.BlockSpec((1, D_MODEL), fixed),
                pl.BlockSpec((ple_dim, D_MODEL), fixed),
                pl.BlockSpec((D_MODEL, D_MODEL), fixed)]
    args = [h, p, g, w_proj, w_gate]
    if final:
        in_specs.append(pl.BlockSpec((1, D_MODEL), fixed))
        args.append(final_g)
    return pl.pallas_call(
        functools.partial(_ple_kernel, final=final),
        grid=(n // tm,),
        in_specs=in_specs,
        out_specs=pl.BlockSpec((tm, D_MODEL), row),
        out_shape=jax.ShapeDtypeStruct((n, D_MODEL), F32),
        compiler_params=_cparams(("parallel",)),
        name="ple_final" if final else "ple",
    )(*args)


def _row_copy(src_ref, src_row, dst_ref, dst_row, sem):
    return pltpu.make_async_copy(src_ref.at[pl.ds(src_row, 1)], dst_ref.at[pl.ds(dst_row, 1)], sem)


def _for_rows(tm, fn):
    def body(g, c):
        base = pl.multiple_of(g * SUBLANES, SUBLANES)
        for k in range(SUBLANES):
            fn(base + k)
        return c

    lax.fori_loop(0, tm // SUBLANES, body, 0)


def _moe_dispatch_kernel(pos_ref, x_ref, xs_in_ref, xs_ref, sem, *, tm):
    del xs_in_ref

    def copies(r):
        return [_row_copy(x_ref, r, xs_ref, pos_ref[s, r], sem) for s in range(TOP_K)]

    def start(r):
        for cp in copies(r):
            cp.start()

    def wait(r):
        for cp in copies(r):
            cp.wait()

    _for_rows(tm, start)
    _for_rows(tm, wait)


def moe_dispatch(xn, pos_tiles, n_rows, tm):
    n = xn.shape[0]
    xs0 = jnp.zeros((n_rows, D_MODEL), F32)
    return pl.pallas_call(
        functools.partial(_moe_dispatch_kernel, tm=tm),
        grid=(n // tm,),
        in_specs=[pl.BlockSpec((None, TOP_K, tm), lambda i: (i, 0, 0), memory_space=pltpu.SMEM),
                  pl.BlockSpec((tm, D_MODEL), lambda i: (i, 0)),
                  pl.BlockSpec(memory_space=pl.ANY)],
        out_specs=pl.BlockSpec(memory_space=pl.ANY),
        out_shape=jax.ShapeDtypeStruct((n_rows, D_MODEL), F32),
        scratch_shapes=[pltpu.SemaphoreType.DMA(())],
        input_output_aliases={2: 0},
        compiler_params=_cparams(("arbitrary",)),
        name="moe_dispatch",
    )(pos_tiles, xn, xs0)


def _moe_grouped_kernel(te_ref, nt_ref, x_ref, wg_ref, wu_ref, wd_ref, o_ref, acc_ref):
    i = pl.program_id(0)
    f = pl.program_id(1)
    live = i < nt_ref[0]

    @pl.when(live & (f == 0))
    def _():
        acc_ref[...] = jnp.zeros_like(acc_ref)

    @pl.when(live)
    def _():
        acc_ref[...] += _swiglu_partial(x_ref[...].astype(BF16), wg_ref[0], wu_ref[0], wd_ref[0])

    @pl.when(live & (f == pl.num_programs(1) - 1))
    def _():
        o_ref[...] = acc_ref[...]

    @pl.when(jnp.logical_not(live) & (f == pl.num_programs(1) - 1))
    def _():
        o_ref[...] = jnp.zeros_like(o_ref)


def moe_grouped(xs, tile_expert, n_tiles, w_gu, w_down, layer, tm, tf):
    n_rows = xs.shape[0]
    max_tiles = n_rows // tm
    d_ff = w_down.shape[2]
    nf = d_ff // tf
    last = lambda i, nt: jnp.minimum(i, nt[0] - 1)
    fblk = lambda i, f, nt: jnp.where(i < nt[0], f, nf - 1)
    row = lambda i, f, te, nt: (last(i, nt), 0)
    grid_spec = pltpu.PrefetchScalarGridSpec(
        num_scalar_prefetch=2,
        grid=(max_tiles, nf),
        in_specs=[pl.BlockSpec((tm, D_MODEL), row),
                  pl.BlockSpec((None, 1, D_MODEL, tf),
                               lambda i, f, te, nt: (layer, te[last(i, nt)], 0, fblk(i, f, nt))),
                  pl.BlockSpec((None, 1, D_MODEL, tf),
                               lambda i, f, te, nt: (layer, te[last(i, nt)], 0, nf + fblk(i, f, nt))),
                  pl.BlockSpec((None, 1, tf, D_MODEL),
                               lambda i, f, te, nt: (layer, te[last(i, nt)], fblk(i, f, nt), 0))],
        out_specs=pl.BlockSpec((tm, D_MODEL), lambda i, f, te, nt: (i, 0)),
        scratch_shapes=[pltpu.VMEM((tm, D_MODEL), F32)])
    return pl.pallas_call(
        _moe_grouped_kernel,
        grid_spec=grid_spec,
        out_shape=jax.ShapeDtypeStruct((n_rows, D_MODEL), F32),
        compiler_params=_cparams(("arbitrary", "arbitrary")),
        name="moe_grouped",
    )(tile_expert, n_tiles, xs, w_gu, w_gu, w_down)


def _moe_combine_kernel(pos_ref, h_ref, route_ref, ys_ref, o_ref, ybuf_ref, sem, *, tm):
    def copies(r):
        return [_row_copy(ys_ref, pos_ref[s, r], ybuf_ref.at[s], r, sem) for s in range(TOP_K)]

    def start(r):
        for cp in copies(r):
            cp.start()

    def wait(r):
        for cp in copies(r):
            cp.wait()

    _for_rows(tm, start)
    route = route_ref[...]
    lane = lax.broadcasted_iota(jnp.int32, route.shape, 1)
    gates = [jnp.sum(jnp.where(lane == LANE_GATE + s, route, 0.0), axis=1, keepdims=True)
             for s in range(TOP_K)]
    _for_rows(tm, wait)
    out = gates[0] * ybuf_ref[0]
    for s in range(1, TOP_K):
        out = out + gates[s] * ybuf_ref[s]
    o_ref[...] = h_ref[...] + out


def moe_combine(h, route, ys, pos_tiles, tm):
    n = h.shape[0]
    row = lambda i: (i, 0)
    return pl.pallas_call(
        functools.partial(_moe_combine_kernel, tm=tm),
        grid=(n // tm,),
        in_specs=[pl.BlockSpec((None, TOP_K, tm), lambda i: (i, 0, 0), memory_space=pltpu.SMEM),
                  pl.BlockSpec((tm, D_MODEL), row),
                  pl.BlockSpec((tm, ROUTER_LANES), row),
                  pl.BlockSpec(memory_space=pl.ANY)],
        out_specs=pl.BlockSpec((tm, D_MODEL), row),
        out_shape=jax.ShapeDtypeStruct((n, D_MODEL), F32),
        scratch_shapes=[pltpu.VMEM((TOP_K, tm, D_MODEL), F32), pltpu.SemaphoreType.DMA(())],
        compiler_params=_cparams(("arbitrary",)),
        name="moe_combine",
    )(pos_tiles, h, route, ys)


def _routing_tables(route, tm):
    n = route.shape[0]
    max_tiles = -(-(TOP_K * n + N_EXPERTS * (tm - 1)) // tm)
    expert = jnp.concatenate([route[:, LANE_EXPERT + s] for s in range(TOP_K)]).astype(jnp.int32)
    onehot = (expert[:, None] == jnp.arange(N_EXPERTS, dtype=jnp.int32)[None, :]).astype(jnp.int32)
    csum = jnp.cumsum(onehot, axis=0)
    counts = csum[-1]
    tiles = (counts + tm - 1) // tm
    tile_end = jnp.cumsum(tiles)
    row_start = (tile_end - tiles) * tm
    pos = jnp.sum(onehot * (csum - 1 + row_start[None, :]), axis=1)
    pos_tiles = pos.reshape(TOP_K, n // tm, tm).transpose(1, 0, 2)
    tile_expert = jnp.sum(jnp.arange(max_tiles, dtype=jnp.int32)[:, None] >= tile_end[None, :], axis=1)
    tile_expert = jnp.minimum(tile_expert, N_EXPERTS - 1).astype(jnp.int32)
    return pos_tiles, tile_expert, tile_end[-1:].astype(jnp.int32), max_tiles * tm


def moe_ffn(h, xn, route, w_gu, w_down, layer, tm, tf):
    pos_tiles, tile_expert, n_tiles, n_rows = _routing_tables(route, tm)
    xs = moe_dispatch(xn, pos_tiles, n_rows, tm)
    ys = moe_grouped(xs, tile_expert, n_tiles, w_gu, w_down, layer, tm, tf)
    return moe_combine(h, route, ys, pos_tiles, tm)


def _later_keys_matrix(tk, with_row_sum):
    j = np.arange(tk)[:, None]
    s = np.arange(2 * tk if with_row_sum else tk)[None, :]
    return jnp.asarray(np.where(s < tk, j > s, True), dtype=BF16)


def _split_bias(bias):
    parts = []
    rest = bias.astype(F32)
    for _ in range(BIAS_PARTS):
        p = rest.astype(BF16).astype(F32)
        parts.append(p)
        rest = rest - p
    return jnp.stack(parts)


def _sb_prompt_kernel(bias_ref, q_ref, k_ref, v_ref, m_ref, o_ref, q4_ref, acc_ref, car_ref, *, tq, tk):
    qi = pl.program_id(1)
    rows = GROUP * tq
    row_g = lax.broadcasted_iota(jnp.int32, (rows, HEAD_DIM), 0) // tq
    lane = lax.broadcasted_iota(jnp.int32, (rows, HEAD_DIM), 1)
    for kvh in range(N_KV_HEADS):
        q4 = jnp.concatenate([q_ref[:, _head_cols(kvh * GROUP + g)] for g in range(GROUP)], axis=0)
        bcols = jnp.zeros((rows, HEAD_DIM), F32)
        for g in range(GROUP):
            for part in range(BIAS_PARTS):
                bcols = jnp.where((row_g == g) & (lane == part), bias_ref[part, kvh * GROUP + g], bcols)
        q4_ref[kvh] = jnp.concatenate([q4, bcols.astype(BF16)], axis=1)
    acc_ref[...] = jnp.zeros_like(acc_ref)
    car_ref[...] = jnp.zeros_like(car_ref)

    def block(j, mask):
        start = pl.multiple_of(j * tk, tk)
        later = m_ref[...]
        for kvh in range(N_KV_HEADS):
            k_t = k_ref[kvh, :, pl.ds(start, tk)]
            v = v_ref[pl.ds(start, tk), kvh * LANES:(kvh + 1) * LANES]
            for r0 in range(0, rows, SB_ROW_CHUNK):
                rs = slice(r0, r0 + SB_ROW_CHUNK)
                z = _dot(q4_ref[kvh, rs], k_t)
                sp = _softplus(z)
                ls = z - sp
                if mask is not None:
                    sp = jnp.where(mask[rs], sp, 0.0)
                sp16 = sp.astype(BF16)
                tail = _dot(sp16, later)
                a = jnp.exp(ls - tail - car_ref[kvh, rs])
                if mask is not None:
                    a = jnp.where(mask[rs], a, 0.0)
                acc_ref[kvh, rs] += _dot(a.astype(BF16), v)
                car_ref[kvh, rs] += tail[:, :1] + sp16[:, :1].astype(F32)

    jd = (qi * tq) // tk
    t_pos = qi * tq + lax.broadcasted_iota(jnp.int32, (rows, tk), 0) % tq
    s_pos = jd * tk + lax.broadcasted_iota(jnp.int32, (rows, tk), 1)
    block(jd, s_pos < t_pos)

    def body(i, c):
        block(jd - 1 - i, None)
        return c

    lax.fori_loop(0, jd, body, 0)

    for kvh in range(N_KV_HEADS):
        for g in range(GROUP):
            o_ref[:, _head_cols(kvh * GROUP + g)] = acc_ref[kvh, g * tq:(g + 1) * tq, :HEAD_DIM].astype(BF16)


def sb_prompt(q, kb, vb, bias, batch, seq, tq, tk):
    nq = seq // tq
    rows = GROUP * tq
    return pl.pallas_call(
        functools.partial(_sb_prompt_kernel, tq=tq, tk=tk),
        grid=(batch, nq),
        in_specs=[pl.BlockSpec(memory_space=pltpu.SMEM),
                  pl.BlockSpec((tq, Q_DIM), lambda b, i: (b * nq + i, 0)),
                  pl.BlockSpec((None, N_KV_HEADS, LANES, seq), lambda b, i: (b, 0, 0, 0)),
                  pl.BlockSpec((seq, N_KV_HEADS * LANES), lambda b, i: (b, 0)),
                  pl.BlockSpec((tk, tk), lambda b, i: (0, 0))],
        out_specs=pl.BlockSpec((tq, Q_DIM), lambda b, i: (b * nq + i, 0)),
        out_shape=jax.ShapeDtypeStruct((batch * seq, Q_DIM), BF16),
        scratch_shapes=[pltpu.VMEM((N_KV_HEADS, rows, LANES), BF16),
                        pltpu.VMEM((N_KV_HEADS, rows, LANES), F32),
                        pltpu.VMEM((N_KV_HEADS, rows, 1), F32)],
        compiler_params=_cparams(("parallel", "parallel")),
        name="sb_prompt",
    )(_split_bias(bias), q, kb, vb, _later_keys_matrix(tk, False))


def _sb_sample_kernel(pt_ref, bias_ref, q_ref, kn_ref, vn_ref, m_ref, pool_k_ref, pool_v_ref, o_ref,
                      q4_ref, bt_ref, acc_ref, car_ref, kbuf_ref, vbuf_ref, sems, *, t, npg, n_pages, layer):
    b = pl.program_id(0)
    c = pl.program_id(1)
    nc = pl.num_programs(1)
    rows = GROUP * t
    hrows = N_KV_HEADS * rows
    later = m_ref[...]
    heads = range(N_KV_HEADS)

    step = b * nc + c
    slot = step % 2

    def page_copies(seq, chunk, slot_):
        cps = []
        for i in range(npg):
            page = pt_ref[seq * n_pages + (n_pages - 1 - (chunk * npg + i))]
            cps.append(pltpu.make_async_copy(pool_k_ref.at[layer, page], kbuf_ref.at[slot_, i], sems.at[0, slot_]))
            cps.append(pltpu.make_async_copy(pool_v_ref.at[layer, page], vbuf_ref.at[slot_, i], sems.at[1, slot_]))
        return cps

    def start_all(cps):
        for j, cp in enumerate(cps):
            cp.start(priority=j // 2 % 2)

    @pl.when(step == 0)
    def _():
        start_all(page_copies(b, c, slot))

    @pl.when(step + 1 < pl.num_programs(0) * nc)
    def _():
        last_chunk = c == nc - 1
        start_all(page_copies(jnp.where(last_chunk, b + 1, b), jnp.where(last_chunk, 0, c + 1), 1 - slot))

    for cp in page_copies(b, c, slot):
        cp.wait()
    k_pages = [kbuf_ref.at[slot, i] for i in range(npg)]
    v_pages = [vbuf_ref.at[slot, i] for i in range(npg)]

    @pl.when(c == 0)
    def _():
        q = q_ref[...].astype(F32)
        row_h = lax.broadcasted_iota(jnp.int32, (hrows, PAGE_SIZE), 0) // t
        bt = jnp.zeros((hrows, PAGE_SIZE), F32)
        for h in range(N_HEADS):
            bt = jnp.where(row_h == h, bias_ref[h], bt)
        bt_ref[...] = bt
        for kvh in heads:
            q4_ref[kvh] = jnp.concatenate(
                [q[:, _head_cols(kvh * GROUP + g)] for g in range(GROUP)], axis=0).astype(BF16)
        acc_ref[...] = jnp.zeros_like(acc_ref)
        car_ref[...] = jnp.zeros_like(car_ref)

    t_in = lax.broadcasted_iota(jnp.int32, (hrows, PAGE_SIZE), 0) % t
    s_in = lax.broadcasted_iota(jnp.int32, (hrows, PAGE_SIZE), 1)
    own = (s_in < t_in) & (c == 0)
    pad = jnp.zeros((PAGE_SIZE - t, HEAD_DIM), F32)

    def own_rows(ref, kvh):
        return jnp.concatenate([ref[:, kvh, :], pad], axis=0).astype(BF16)

    def pages_cat(refs, kvh):
        return jnp.concatenate([r[kvh].astype(BF16) for r in refs], axis=1)

    def own_masked(x):
        return jnp.concatenate([jnp.where(own, x[:hrows], 0.0), x[hrows:]], axis=0)

    bt = bt_ref[...]
    z_own = jnp.concatenate([_dot_nt(q4_ref[kvh], own_rows(kn_ref, kvh)) for kvh in heads], axis=0) + bt
    zc = [_dot(q4_ref[kvh], pages_cat(k_pages, kvh)) for kvh in heads]
    z = jnp.concatenate(
        [z_own] +
        [jnp.concatenate([zc[kvh][:, i * PAGE_SIZE:(i + 1) * PAGE_SIZE] for kvh in heads], axis=0) + bt
         for i in range(npg)], axis=0)
    sp = own_masked(_softplus(z))
    res = _dot(sp.astype(BF16), later)
    inc = res[:, PAGE_SIZE:]
    carry = car_ref[...]
    carries = []
    for i in range(npg + 1):
        carries.append(carry)
        carry = carry + inc[i * hrows:(i + 1) * hrows]
    car_ref[...] = carry
    a = own_masked(jnp.exp(z - sp - res[:, :PAGE_SIZE] - jnp.concatenate(carries, axis=0))).astype(BF16)
    for kvh in heads:
        lo = kvh * rows
        a_cat = jnp.concatenate(
            [a[(i + 1) * hrows + lo:(i + 1) * hrows + lo + rows] for i in range(npg)], axis=1)
        acc_ref[kvh] += (_dot(a[lo:lo + rows], own_rows(vn_ref, kvh))
                         + _dot_nt(a_cat, pages_cat(v_pages, kvh)))

    @pl.when(c == pl.num_programs(1) - 1)
    def _():
        for kvh in heads:
            for g in range(GROUP):
                o_ref[:, _head_cols(kvh * GROUP + g)] = acc_ref[kvh, g * t:(g + 1) * t, :].astype(BF16)


def sb_sample(q_s, k_new, v_new, pool_k, pool_v, page_table, layer, bias):
    db, t, _ = q_s.shape
    n_pages = page_table.shape[1]
    npg = PAGES_PER_STEP
    assert n_pages % npg == 0, (n_pages, npg)
    nc = n_pages // npg
    rows = GROUP * t
    seq_blk = lambda b, c, pt: (b, 0, 0)
    seq_blk4 = lambda b, c, pt: (b, 0, 0, 0)

    page_buf = pltpu.VMEM((2, npg, N_KV_HEADS, HEAD_DIM, PAGE_SIZE), F32)
    grid_spec = pltpu.PrefetchScalarGridSpec(
        num_scalar_prefetch=1,
        grid=(db, nc),
        in_specs=[pl.BlockSpec(memory_space=pltpu.SMEM),
                  pl.BlockSpec((None, t, Q_DIM), seq_blk),
                  pl.BlockSpec((None, t, N_KV_HEADS, HEAD_DIM), seq_blk4),
                  pl.BlockSpec((None, t, N_KV_HEADS, HEAD_DIM), seq_blk4),
                  pl.BlockSpec((PAGE_SIZE, 2 * PAGE_SIZE), lambda b, c, pt: (0, 0)),
                  pl.BlockSpec(memory_space=pl.ANY),
                  pl.BlockSpec(memory_space=pl.ANY)],
        out_specs=pl.BlockSpec((None, t, Q_DIM), seq_blk),
        scratch_shapes=[pltpu.VMEM((N_KV_HEADS, rows, HEAD_DIM), BF16),
                        pltpu.VMEM((N_KV_HEADS * rows, PAGE_SIZE), F32),
                        pltpu.VMEM((N_KV_HEADS, rows, HEAD_DIM), F32),
                        pltpu.VMEM((N_KV_HEADS * rows, PAGE_SIZE), F32),
                        page_buf, page_buf,
                        pltpu.SemaphoreType.DMA((2, 2))])
    return pl.pallas_call(
        functools.partial(_sb_sample_kernel, t=t, npg=npg, n_pages=n_pages, layer=layer),
        grid_spec=grid_spec,
        out_shape=jax.ShapeDtypeStruct((db, t, Q_DIM), BF16),
        compiler_params=_cparams(("arbitrary", "arbitrary")),
        name="sb_sample",
    )(page_table.reshape(-1), bias, q_s, k_new, v_new, _later_keys_matrix(PAGE_SIZE, True), pool_k, pool_v)


def _alibi_slopes():
    h = np.arange(1, N_HEADS + 1, dtype=np.float64)
    return np.exp2(-8.0 * h / N_HEADS).reshape(N_KV_HEADS, GROUP)


def _swa_tile(t_q, dist):
    valid = (dist >= 0) & (dist < WINDOW)
    tile = np.where(valid[None, None], -_alibi_slopes()[:, :, None, None] * dist[None, None], -np.inf)
    return jnp.asarray(tile.reshape(N_KV_HEADS, GROUP * t_q, dist.shape[1]), dtype=F32)


def _swa_probs(s, sink):
    m = jnp.maximum(jnp.max(s, axis=1, keepdims=True), sink)
    p = jnp.exp(s - m)
    denom = jnp.sum(p, axis=1, keepdims=True) + jnp.exp(sink - m)
    return (p * (1.0 / denom)).astype(BF16)


def _sink_col(sinks_ref, kvh, rows, t):
    row_g = lax.broadcasted_iota(jnp.int32, (rows, 1), 0) // t
    col = jnp.zeros((rows, 1), F32)
    for g in range(GROUP):
        col = jnp.where(row_g == g, sinks_ref[kvh * GROUP + g], col)
    return col


def _swa_prompt_kernel(sinks_ref, q_ref, kp_ref, kc_ref, vp_ref, vc_ref, tile_ref, o_ref, *, tq):
    qi = pl.program_id(1)
    rows = GROUP * tq
    col = lax.broadcasted_iota(jnp.int32, (rows, 2 * tq), 1)
    keep = (col >= tq) | (qi > 0)
    vband = jnp.concatenate([vp_ref[...], vc_ref[...]], axis=0)
    for kvh in range(N_KV_HEADS):
        q4 = jnp.concatenate([q_ref[:, _head_cols(kvh * GROUP + g)] for g in range(GROUP)], axis=0)
        kband_t = jnp.concatenate([kp_ref[kvh, :HEAD_DIM], kc_ref[kvh, :HEAD_DIM]], axis=1)
        s = _dot(q4, kband_t) + jnp.where(keep, tile_ref[kvh], NEG_INF)
        o = _dot(_swa_probs(s, _sink_col(sinks_ref, kvh, rows, tq)),
                 vband[:, kvh * LANES:kvh * LANES + HEAD_DIM])
        for g in range(GROUP):
            o_ref[:, _head_cols(kvh * GROUP + g)] = o[g * tq:(g + 1) * tq].astype(BF16)


def swa_prompt(q, kb, vb, sinks, batch, seq):
    tq = WINDOW
    nq = seq // tq
    qi = np.arange(tq)[:, None]
    kj = np.arange(2 * tq)[None, :]
    tile = _swa_tile(tq, tq + qi - kj)
    cur = lambda b, i: (b * nq + i, 0)
    prev = lambda b, i: (b * nq + jnp.maximum(i - 1, 0), 0)
    cur_t = lambda b, i: (b, 0, 0, i)
    prev_t = lambda b, i: (b, 0, 0, jnp.maximum(i - 1, 0))
    return pl.pallas_call(
        functools.partial(_swa_prompt_kernel, tq=tq),
        grid=(batch, nq),
        in_specs=[pl.BlockSpec(memory_space=pltpu.SMEM),
                  pl.BlockSpec((tq, Q_DIM), cur),
                  pl.BlockSpec((None, N_KV_HEADS, LANES, tq), prev_t),
                  pl.BlockSpec((None, N_KV_HEADS, LANES, tq), cur_t),
                  pl.BlockSpec((tq, N_KV_HEADS * LANES), prev),
                  pl.BlockSpec((tq, N_KV_HEADS * LANES), cur),
                  pl.BlockSpec((N_KV_HEADS, GROUP * tq, 2 * tq), lambda b, i: (0, 0, 0))],
        out_specs=pl.BlockSpec((tq, Q_DIM), cur),
        out_shape=jax.ShapeDtypeStruct((batch * seq, Q_DIM), BF16),
        compiler_params=_cparams(("parallel", "parallel")),
        name="swa_prompt",
    )(sinks, q, kb, kb, vb, vb, tile)


def _swa_sample_kernel(sinks_ref, q_ref, kn_ref, vn_ref, kbuf_ref, vbuf_ref, tile_ref,
                       o_ref, ko_ref, vo_ref, *, t, w_buf):
    rows = GROUP * t
    row_pad = jnp.zeros((w_buf - t, HEAD_DIM), F32)
    lane_pad = jnp.zeros((w_buf, LANES - HEAD_DIM), F32)
    sink_cols = [_sink_col(sinks_ref, kvh, rows, t) for kvh in range(N_KV_HEADS)]

    def shifted(buf_t, new):
        new_t = jnp.concatenate([new, lane_pad], axis=1).T[:HEAD_DIM, :t]
        return jnp.concatenate([buf_t[:, t:], new_t], axis=1)

    for s in range(q_ref.shape[0]):
        q = q_ref[s].astype(F32)
        for kvh in range(N_KV_HEADS):
            kbuf_t = kbuf_ref[s, kvh]
            vbuf_t = vbuf_ref[s, kvh]
            kn = jnp.concatenate([kn_ref[s, :, kvh, :], row_pad], axis=0)
            vn = jnp.concatenate([vn_ref[s, :, kvh, :], row_pad], axis=0)
            ko_ref[s, kvh] = shifted(kbuf_t, kn)
            vo_ref[s, kvh] = shifted(vbuf_t, vn)
            q4 = jnp.concatenate(
                [q[:, _head_cols(kvh * GROUP + g)] for g in range(GROUP)], axis=0).astype(BF16)
            sc = jnp.concatenate([_dot(q4, kbuf_t.astype(BF16)), _dot_nt(q4, kn.astype(BF16))], axis=1)
            p = _swa_probs(sc + tile_ref[kvh], sink_cols[kvh])
            o = _dot_nt(p[:, :w_buf], vbuf_t.astype(BF16)) + _dot(p[:, w_buf:], vn.astype(BF16))
            for g in range(GROUP):
                o_ref[s, :, _head_cols(kvh * GROUP + g)] = o[g * t:(g + 1) * t].astype(BF16)


def swa_sample(q_s, k_new, v_new, buf_k, buf_v, layer, sinks):
    db, t, _ = q_s.shape
    w_buf = buf_k.shape[-1]
    ns = SWA_SEQS_PER_STEP
    idx = np.arange(2 * w_buf)[None, :]
    dist = np.where(idx < w_buf + t, (w_buf + np.arange(t))[:, None] - idx, -1)
    tile = _swa_tile(t, dist)
    seq_blk = lambda i: (i, 0, 0)
    seq_blk4 = lambda i: (i, 0, 0, 0)
    buf_blk = lambda i: (layer, i, 0, 0, 0)
    kv_shape = (ns, N_KV_HEADS, HEAD_DIM, w_buf)
    return pl.pallas_call(
        functools.partial(_swa_sample_kernel, t=t, w_buf=w_buf),
        grid=(db // ns,),
        in_specs=[pl.BlockSpec(memory_space=pltpu.SMEM),
                  pl.BlockSpec((ns, t, Q_DIM), seq_blk),
                  pl.BlockSpec((ns, t, N_KV_HEADS, HEAD_DIM), seq_blk4),
                  pl.BlockSpec((ns, t, N_KV_HEADS, HEAD_DIM), seq_blk4),
                  pl.BlockSpec((None,) + kv_shape, buf_blk),
                  pl.BlockSpec((None,) + kv_shape, buf_blk),
                  pl.BlockSpec((N_KV_HEADS, GROUP * t, 2 * w_buf), lambda i: (0, 0, 0))],
        out_specs=[pl.BlockSpec((ns, t, Q_DIM), seq_blk),
                   pl.BlockSpec(kv_shape, seq_blk4),
                   pl.BlockSpec(kv_shape, seq_blk4)],
        out_shape=[jax.ShapeDtypeStruct((db, t, Q_DIM), BF16),
                   jax.ShapeDtypeStruct((db, N_KV_HEADS, HEAD_DIM, w_buf), F32),
                   jax.ShapeDtypeStruct((db, N_KV_HEADS, HEAD_DIM, w_buf), F32)],
        compiler_params=_cparams(("parallel",)),
        name="swa_sample",
    )(sinks, q_s, k_new, v_new, buf_k, buf_v, tile)


def _pad_router(w):
    w = jnp.pad(w, ((0, 0), (0, ROUTER_LANES - w.shape[1])))
    hi = w.astype(BF16)
    return hi, (w - hi.astype(F32)).astype(BF16)


def _token_tile(n_p, n_s):
    for tm in (512, 256, 128, 64, 32, 16, 8):
        if n_p % tm == 0 and n_s % tm == 0:
            return tm
    raise ValueError(f"token counts {n_p}, {n_s} are not multiples of 8")


def kernel(x_prompt, x_sample, cache_sb_k, cache_sb_v, cache_swa_k, cache_swa_v, page_table,
           p_prompt, p_sample, attn_norm, w_qkv, w_o, sb_bias, attn_sinks, ffn_norm, w_gu_dense,
           w_down_dense, w_router, w_gu_expert, w_down_expert, ple_norm, w_ple_proj,
           w_ple_gate, final_norm):
    batch, seq, _ = x_prompt.shape
    db, t, _ = x_sample.shape
    depth = attn_norm.shape[0]
    n_p = batch * seq
    n_s = db * t
    tm = _token_tile(n_p, n_s)
    assert seq % SB_KEY_BLOCK == 0 and seq % WINDOW == 0

    h = jnp.concatenate([x_prompt.reshape(n_p, D_MODEL), x_sample.reshape(n_s, D_MODEL)], axis=0)
    p_all = jnp.concatenate([p_prompt.reshape(depth, n_p, -1), p_sample.reshape(depth, n_s, -1)], axis=1)
    w_keep = min(WINDOW, seq)
    d_ff = w_down_dense.shape[1]
    d_ffe = w_down_expert.shape[2]
    to_t = lambda x: jnp.moveaxis(x, -3, -1)
    from_t = lambda x: jnp.moveaxis(x, -1, -3)
    pool_k, pool_v = to_t(cache_sb_k), to_t(cache_sb_v)
    buf_k, buf_v = to_t(cache_swa_k), to_t(cache_swa_v)

    w_gu_e, w_down_e = w_gu_expert.astype(BF16), w_down_expert.astype(BF16)

    sb_kp, sb_vp, sb_ks, sb_vs = [], [], [], []
    sw_kp, sw_vp, sw_ks, sw_vs = [], [], [], []
    for i in range(depth):
        j = i // 2
        g_attn = attn_norm[i][None]
        w_i = w_qkv[i].astype(BF16)
        q_p, kt_p, vt_p, kb, vb = qkv_prompt(h, g_attn, w_i, batch, seq, tm)
        q_s, k_s, v_s = qkv_sample(h, g_attn, w_i, n_p, n_s, tm)
        q_s = q_s.reshape(db, t, Q_DIM)
        k_s = k_s.reshape(db, t, N_KV_HEADS, HEAD_DIM)
        v_s = v_s.reshape(db, t, N_KV_HEADS, HEAD_DIM)
        if i % 2 == 0:
            o_p = sb_prompt(q_p, kb, vb, sb_bias[j], batch, seq, SB_QUERY_BLOCK, SB_KEY_BLOCK)
            o_s = sb_sample(q_s, k_s, v_s, pool_k, pool_v, page_table, j, sb_bias[j])
            sb_kp.append(from_t(kt_p)); sb_vp.append(from_t(vt_p)); sb_ks.append(k_s); sb_vs.append(v_s)
        else:
            o_p = swa_prompt(q_p, kb, vb, attn_sinks[j], batch, seq)
            o_s, bks, bvs = swa_sample(q_s, k_s, v_s, buf_k, buf_v, j, attn_sinks[j])
            sw_kp.append(from_t(kt_p[..., seq - w_keep:])); sw_vp.append(from_t(vt_p[..., seq - w_keep:]))
            sw_ks.append(from_t(bks)); sw_vs.append(from_t(bvs))
        o_s = o_s.reshape(n_s, Q_DIM)
        if i % 2 == 0:
            h1, xn = post_attn(h, o_p, o_s, w_o[i].astype(BF16), ffn_norm[i][None], None, tm)
            h2 = dense_ffn(h1, xn, w_gu_dense[j].astype(BF16), w_down_dense[j].astype(BF16), tm, d_ff)
        else:
            h1, xn, route = post_attn(h, o_p, o_s, w_o[i].astype(BF16), ffn_norm[i][None],
                                      _pad_router(w_router[j]), tm)
            h2 = moe_ffn(h1, xn, route, w_gu_e, w_down_e, j, tm, d_ffe // 2)
        ple_args = (h2, p_all[i], ple_norm[i][None], w_ple_proj[i].astype(BF16), w_ple_gate[i].astype(BF16))
        if i < depth - 1:
            h = ple_add(*ple_args, None, tm)
        else:
            y_p = ple_add(*ple_args, final_norm[None], tm, 0, n_p)
            y_s = ple_add(*ple_args, final_norm[None], tm, n_p, n_s)

    y_prompt = y_p.reshape(batch, seq, D_MODEL)
    y_sample = y_s.reshape(db, t, D_MODEL)
    return (y_prompt, y_sample,
            jnp.stack(sb_kp), jnp.stack(sb_vp), jnp.stack(sb_ks), jnp.stack(sb_vs),
            jnp.stack(sw_kp), jnp.stack(sw_vp), jnp.stack(sw_ks), jnp.stack(sw_vs))
```
